```python
import math
import jax
import jax.numpy as jnp
from jax import lax
import numpy as np


D_MODEL = 2048
BATCH = 1
SEQ = 16384
DEPTH = 4

HEAD_DIM = 64
BRANCH_WIDTH = D_MODEL // 4
MIX_WIDTH = 4 * BRANCH_WIDTH
A_HEADS = BRANCH_WIDTH // HEAD_DIM
DILATED_PATTERNS = ((128, 1), (512, 4), (2048, 16))
ROPE_THETA = 500000.0
ROT_DIMS = HEAD_DIM // 4
SSM_GROUP = 16
SSM_GROUPS = BRANCH_WIDTH // SSM_GROUP
SSM_STATE = 64
C_HEADS = BRANCH_WIDTH // HEAD_DIM
C_KV_HEADS = C_HEADS // 4
AXIAL_THETA = 10000.0
GRID_W = 64
D_HEADS = BRANCH_WIDTH // (2 * HEAD_DIM)
PLE_DIM = 256
BLOCK_Q = 128
NORM_EPS = 1e-6
MASK_VALUE = -1e30

SPLIT_SIZES = (
    BRANCH_WIDTH, BRANCH_WIDTH, BRANCH_WIDTH, BRANCH_WIDTH,
    BRANCH_WIDTH, BRANCH_WIDTH,
    BRANCH_WIDTH, C_KV_HEADS * HEAD_DIM, C_KV_HEADS * HEAD_DIM, BRANCH_WIDTH,
    BRANCH_WIDTH, BRANCH_WIDTH, BRANCH_WIDTH, BRANCH_WIDTH,
)
IN_COLS = sum(SPLIT_SIZES)
SPLIT_POINTS = tuple(int(c) for c in np.cumsum(SPLIT_SIZES)[:-1])

kernel_name = 'hybrid_parallel_head_encoder'


def rms_norm(x, w):
    xf = x.astype(jnp.float32)
    y = xf * lax.rsqrt(jnp.mean(xf * xf, axis=-1, keepdims=True) + NORM_EPS)
    return (y * w.astype(jnp.float32)).astype(x.dtype)


def rope_tables(pos, n_dims, theta):
    inv = theta ** (-jnp.arange(0, n_dims, 2, dtype=jnp.float32) / n_dims)
    ang = pos[:, None] * inv[None, :]
    return jnp.cos(ang), jnp.sin(ang)


def rotate(x, cos, sin):
    x1, x2 = jnp.split(x.astype(jnp.float32), 2, axis=-1)
    c = cos[None, :, None, :]
    s = sin[None, :, None, :]
    return jnp.concatenate([x1 * c - x2 * s, x1 * s + x2 * c], axis=-1).astype(x.dtype)


def partial_rope(x, cos, sin):
    return jnp.concatenate([rotate(x[..., :ROT_DIMS], cos, sin), x[..., ROT_DIMS:]], axis=-1)


def axial_rope(x, cos_r, sin_r, cos_c, sin_c):
    half = HEAD_DIM // 2
    return jnp.concatenate([rotate(x[..., :half], cos_r, sin_r),
                            rotate(x[..., half:], cos_c, sin_c)], axis=-1)


def banded_attention(q, k, v, n_side):
    L, E = q.shape[-2], q.shape[-1]
    lead = q.shape[:-2]
    bq = min(BLOCK_Q, L)
    nb = -(-L // bq)
    lp = nb * bq

    def pad(t, lo, hi):
        return jnp.pad(t, [(0, 0)] * len(lead) + [(lo, hi), (0, 0)])

    qb = pad(q, 0, lp - L).reshape(*lead, nb, bq, E)
    kw = bq + 2 * n_side
    idx = (jnp.arange(nb) * bq)[:, None] + jnp.arange(kw)[None, :]
    kb = jnp.take(pad(k, n_side, n_side + lp - L), idx, axis=-2)
    vb = jnp.take(pad(v, n_side, n_side + lp - L), idx, axis=-2)
    key_pos = idx - n_side
    q_pos = (jnp.arange(nb) * bq)[:, None] + jnp.arange(bq)[None, :]
    rel = key_pos[:, None, :] - q_pos[:, :, None]
    valid = (jnp.abs(rel) <= n_side) & (key_pos >= 0)[:, None, :] & (key_pos < L)[:, None, :]
    s = jnp.einsum('...qe,...ke->...qk', qb, kb).astype(jnp.float32) * (E ** -0.5)
    s = jnp.where(valid, s, MASK_VALUE)
    lse = jax.nn.logsumexp(s, axis=-1)
    pr = jnp.exp(s - lse[..., None]).astype(v.dtype)
    o = jnp.einsum('...qk,...ke->...qe', pr, vb)
    o = o.reshape(*lead, lp, E)[..., :L, :]
    lse = lse.reshape(*lead, lp)[..., :L]
    return o, lse


def residue_split(t, dil):
    b, h, s, e = t.shape
    return t.reshape(b, h, s // dil, dil, e).swapaxes(2, 3)


def dilated_attention(q, k, v):
    bsz, s_len, h, dh = q.shape
    qt, kt, vt = (t.transpose(0, 2, 1, 3) for t in (q, k, v))
    outs, lses = [], []
    for window, dil in DILATED_PATTERNS:
        n_side = (window // 2) // dil
        o, lse = banded_attention(residue_split(qt, dil), residue_split(kt, dil),
                                  residue_split(vt, dil), n_side)
        outs.append(o.swapaxes(2, 3).reshape(bsz, h, s_len, dh))
        lses.append(lse.swapaxes(2, 3).reshape(bsz, h, s_len))
    wts = jax.nn.softmax(jnp.stack(lses), axis=0)
    o = jnp.sum(wts[..., None] * jnp.stack(outs).astype(jnp.float32), axis=0).astype(q.dtype)
    return o.transpose(0, 2, 1, 3).reshape(bsz, s_len, h * dh)


def complex_linear_combine(e1, e2):
    a1r, a1i, b1r, b1i = e1
    a2r, a2i, b2r, b2i = e2
    return (a2r * a1r - a2i * a1i,
            a2r * a1i + a2i * a1r,
            a2r * b1r - a2i * b1i + b2r,
            a2r * b1i + a2i * b1r + b2i)


def s5_mixer(u, lam_re, lam_im, log_dt, b_re, b_im, c_re, c_im, d_skip, w_glu, b_glu):
    bsz, s_len, width = u.shape
    f32 = jnp.float32
    uf = u.astype(f32)
    ug = uf.reshape(bsz, s_len, SSM_GROUPS, SSM_GROUP)
    lr = lam_re.astype(f32)
    li = lam_im.astype(f32)
    dt = jnp.exp(log_dt.astype(f32))[..., None]
    mag = jnp.exp(lr * dt)
    ar = mag * jnp.cos(li * dt)
    ai = mag * jnp.sin(li * dt)
    den = lr * lr + li * li
    coef_re = ((ar - 1.0) * lr + ai * li) / den
    coef_im = (ai * lr - (ar - 1.0) * li) / den
    br = b_re.astype(f32)
    bi = b_im.astype(f32)
    bbar_re = coef_re[..., None] * br - coef_im[..., None] * bi
    bbar_im = coef_re[..., None] * bi + coef_im[..., None] * br
    y = d_skip.astype(f32) * uf
    for direction in range(2):
        ud = ug if direction == 0 else jnp.flip(ug, axis=1)
        bu_re = jnp.einsum('gpc,bsgc->bsgp', bbar_re[direction], ud)
        bu_im = jnp.einsum('gpc,bsgc->bsgp', bbar_im[direction], ud)
        a_re = jnp.broadcast_to(ar[direction], bu_re.shape)
        a_im = jnp.broadcast_to(ai[direction], bu_re.shape)
        _, _, xr, xi = lax.associative_scan(complex_linear_combine, (a_re, a_im, bu_re, bu_im), axis=1)
        yd = (jnp.einsum('gcp,bsgp->bsgc', c_re[direction].astype(f32), xr)
              - jnp.einsum('gcp,bsgp->bsgc', c_im[direction].astype(f32), xi))
        if direction == 1:
            yd = jnp.flip(yd, axis=1)
        y = y + yd.reshape(bsz, s_len, width)
    y = jax.nn.gelu(y)
    z = y @ w_glu.astype(f32) + b_glu.astype(f32)
    out = z[..., :width] * jax.nn.sigmoid(z[..., width:])
    return out.astype(u.dtype)


def gqa_attention(q, k, v):
    bsz, s_len, hq, dh = q.shape
    hkv = k.shape[2]
    grp = hq // hkv
    nb = s_len // BLOCK_Q
    qb = q.reshape(bsz, s_len, hkv, grp, dh).transpose(0, 2, 3, 1, 4)
    qb = jnp.moveaxis(qb.reshape(bsz, hkv, grp, nb, BLOCK_Q, dh), 3, 0)
    kt = k.transpose(0, 2, 1, 3)
    vt = v.transpose(0, 2, 1, 3)
    scale = dh ** -0.5

    def block(qblk):
        s = jnp.einsum('bhgqd,bhkd->bhgqk', qblk, kt).astype(jnp.float32) * scale
        pr = jax.nn.softmax(s, axis=-1).astype(vt.dtype)
        return jnp.einsum('bhgqk,bhkd->bhgqd', pr, vt)

    o = lax.map(block, qb)
    o = jnp.moveaxis(o, 0, 3).reshape(bsz, hkv, grp, s_len, dh).transpose(0, 3, 1, 2, 4)
    return o.reshape(bsz, s_len, hq * dh)


def diff_attention(q, k, v, lam, subln_w, lam_init):
    bsz, s_len, h, _, dh = q.shape
    nb = s_len // BLOCK_Q
    qb = q.transpose(0, 2, 3, 1, 4).reshape(bsz, h, 2, nb, BLOCK_Q, dh)
    qb = jnp.moveaxis(qb, 3, 0)
    kt = k.transpose(0, 2, 3, 1, 4)
    vt = v.transpose(0, 2, 1, 3)
    scale = dh ** -0.5

    def block(qblk):
        s = jnp.einsum('bhcqd,bhckd->bhcqk', qblk, kt).astype(jnp.float32) * scale
        pr = jax.nn.softmax(s, axis=-1)
        attn = (pr[:, :, 0] - lam * pr[:, :, 1]).astype(vt.dtype)
        return jnp.einsum('bhqk,bhkd->bhqd', attn, vt)

    o = lax.map(block, qb)
    o = jnp.moveaxis(o, 0, 2).reshape(bsz, h, s_len, 2 * dh)
    o = rms_norm(o, subln_w) * (1.0 - lam_init)
    return o.transpose(0, 2, 1, 3).reshape(bsz, s_len, h * 2 * dh)


def setup_inputs(seed: int = 0) -> dict:
    key = jax.random.key(seed)
    ks = iter(jax.random.split(key, 32))
    f32 = jnp.float32

    def nrm(shape, scale):
        return scale * jax.random.normal(next(ks), shape, f32)

    L = DEPTH
    x = nrm((BATCH, SEQ, D_MODEL), 1.0)
    p = nrm((DEPTH, BATCH, SEQ, PLE_DIM), 1.0)
    norm_w = 1.0 + nrm((L, D_MODEL), 0.02)
    w_in = nrm((L, D_MODEL, IN_COLS), D_MODEL ** -0.5)
    w_out = nrm((L, MIX_WIDTH, D_MODEL), 0.5 * MIX_WIDTH ** -0.5)
    a_q_norm = 1.0 + nrm((L, HEAD_DIM), 0.02)
    a_k_norm = 1.0 + nrm((L, HEAD_DIM), 0.02)
    s5_lambda_re = -0.5 * jnp.exp(nrm((L, 2, SSM_GROUPS, SSM_STATE), 0.05))
    s5_lambda_im = math.pi * jnp.arange(SSM_STATE, dtype=f32) + nrm((L, 2, SSM_GROUPS, SSM_STATE), 0.01)
    s5_log_dt = jax.random.uniform(next(ks), (L, 2, SSM_GROUPS), f32,
                                   minval=math.log(1e-3), maxval=math.log(1e-1))
    s5_b_re = nrm((L, 2, SSM_GROUPS, SSM_STATE, SSM_GROUP), (2 * SSM_GROUP) ** -0.5)
    s5_b_im = nrm((L, 2, SSM_GROUPS, SSM_STATE, SSM_GROUP), (2 * SSM_GROUP) ** -0.5)
    s5_c_re = nrm((L, 2, SSM_GROUPS, SSM_GROUP, SSM_STATE), (2 * SSM_STATE) ** -0.5)
    s5_c_im = nrm((L, 2, SSM_GROUPS, SSM_GROUP, SSM_STATE), (2 * SSM_STATE) ** -0.5)
    s5_d = nrm((L, BRANCH_WIDTH), 1.0)
    s5_w_glu = nrm((L, BRANCH_WIDTH, 2 * BRANCH_WIDTH), BRANCH_WIDTH ** -0.5)
    s5_b_glu = nrm((L, 2 * BRANCH_WIDTH), 0.01)
    c_q_norm = 1.0 + nrm((L, HEAD_DIM), 0.02)
    c_k_norm = 1.0 + nrm((L, HEAD_DIM), 0.02)
    d_q_norm = 1.0 + nrm((L, HEAD_DIM), 0.02)
    d_k_norm = 1.0 + nrm((L, HEAD_DIM), 0.02)
    d_lambda_q1 = nrm((L, HEAD_DIM), 0.1)
    d_lambda_k1 = nrm((L, HEAD_DIM), 0.1)
    d_lambda_q2 = nrm((L, HEAD_DIM), 0.1)
    d_lambda_k2 = nrm((L, HEAD_DIM), 0.1)
    d_subln = 1.0 + nrm((L, 2 * HEAD_DIM), 0.02)
    ple_norm_w = 1.0 + nrm((L, D_MODEL), 0.02)
    ple_gate_w = nrm((L, D_MODEL, D_MODEL), D_MODEL ** -0.5)
    ple_w = nrm((L, PLE_DIM, D_MODEL), 0.5 * PLE_DIM ** -0.5)
    return {'x': x, 'p': p, 'norm_w': norm_w, 'w_in': w_in, 'w_out': w_out,
            'a_q_norm': a_q_norm, 'a_k_norm': a_k_norm,
            's5_lambda_re': s5_lambda_re, 's5_lambda_im': s5_lambda_im, 's5_log_dt': s5_log_dt,
            's5_b_re': s5_b_re, 's5_b_im': s5_b_im, 's5_c_re': s5_c_re, 's5_c_im': s5_c_im,
            's5_d': s5_d, 's5_w_glu': s5_w_glu, 's5_b_glu': s5_b_glu,
            'c_q_norm': c_q_norm, 'c_k_norm': c_k_norm,
            'd_q_norm': d_q_norm, 'd_k_norm': d_k_norm,
            'd_lambda_q1': d_lambda_q1, 'd_lambda_k1': d_lambda_k1,
            'd_lambda_q2': d_lambda_q2, 'd_lambda_k2': d_lambda_k2, 'd_subln': d_subln,
            'ple_norm_w': ple_norm_w, 'ple_gate_w': ple_gate_w, 'ple_w': ple_w}


def reference(x, p, norm_w, w_in, w_out, a_q_norm, a_k_norm,
              s5_lambda_re, s5_lambda_im, s5_log_dt, s5_b_re, s5_b_im, s5_c_re, s5_c_im,
              s5_d, s5_w_glu, s5_b_glu, c_q_norm, c_k_norm, d_q_norm, d_k_norm,
              d_lambda_q1, d_lambda_k1, d_lambda_q2, d_lambda_k2, d_subln,
              ple_norm_w, ple_gate_w, ple_w):
    f32 = jnp.float32
    bsz, s_len = x.shape[0], x.shape[1]
    t = jnp.arange(s_len)
    cos_1d, sin_1d = rope_tables(t.astype(f32), ROT_DIMS, ROPE_THETA)
    rows = s_len // GRID_W
    row_c = (t // GRID_W - rows // 2).astype(f32)
    col_c = (t % GRID_W - GRID_W // 2).astype(f32)
    cos_r, sin_r = rope_tables(row_c, HEAD_DIM // 2, AXIAL_THETA)
    cos_c, sin_c = rope_tables(col_c, HEAD_DIM // 2, AXIAL_THETA)

    for i in range(DEPTH):
        h = rms_norm(x, norm_w[i])
        (aq, ak, av, ag, bu, bg, cq, ck, cv, cg,
         dq, dk, dv, dg) = jnp.split(h @ w_in[i], SPLIT_POINTS, axis=-1)

        aq = partial_rope(rms_norm(aq.reshape(bsz, s_len, A_HEADS, HEAD_DIM), a_q_norm[i]), cos_1d, sin_1d)
        ak = partial_rope(rms_norm(ak.reshape(bsz, s_len, A_HEADS, HEAD_DIM), a_k_norm[i]), cos_1d, sin_1d)
        a_out = dilated_attention(aq, ak, av.reshape(bsz, s_len, A_HEADS, HEAD_DIM))

        b_out = s5_mixer(bu, s5_lambda_re[i], s5_lambda_im[i], s5_log_dt[i], s5_b_re[i], s5_b_im[i],
                         s5_c_re[i], s5_c_im[i], s5_d[i], s5_w_glu[i], s5_b_glu[i])

        cq = axial_rope(rms_norm(cq.reshape(bsz, s_len, C_HEADS, HEAD_DIM), c_q_norm[i]),
                        cos_r, sin_r, cos_c, sin_c)
        ck = axial_rope(rms_norm(ck.reshape(bsz, s_len, C_KV_HEADS, HEAD_DIM), c_k_norm[i]),
                        cos_r, sin_r, cos_c, sin_c)
        c_out = gqa_attention(cq, ck, cv.reshape(bsz, s_len, C_KV_HEADS, HEAD_DIM))

        dq = partial_rope(rms_norm(dq.reshape(bsz, s_len, 2 * D_HEADS, HEAD_DIM), d_q_norm[i]),
                          cos_1d, sin_1d).reshape(bsz, s_len, D_HEADS, 2, HEAD_DIM)
        dk = partial_rope(rms_norm(dk.reshape(bsz, s_len, 2 * D_HEADS, HEAD_DIM), d_k_norm[i]),
                          cos_1d, sin_1d).reshape(bsz, s_len, D_HEADS, 2, HEAD_DIM)
        lam_init = 0.8 - 0.6 * math.exp(-0.3 * i)
        lam = (jnp.exp(jnp.sum(d_lambda_q1[i].astype(f32) * d_lambda_k1[i].astype(f32)))
               - jnp.exp(jnp.sum(d_lambda_q2[i].astype(f32) * d_lambda_k2[i].astype(f32))) + lam_init)
        d_out = diff_attention(dq, dk, dv.reshape(bsz, s_len, D_HEADS, 2 * HEAD_DIM), lam, d_subln[i], lam_init)

        mixed = jnp.concatenate([a_out * jax.nn.silu(ag), b_out * jax.nn.silu(bg),
                                 c_out * jax.nn.silu(cg), d_out * jax.nn.silu(dg)], axis=-1)
        x = x + mixed @ w_out[i]

        gate = jax.nn.sigmoid(rms_norm(x, ple_norm_w[i]) @ ple_gate_w[i])
        x = x + gate * (p[i] @ ple_w[i])
    return x
```

```python
import functools
import math

import jax
import jax.numpy as jnp
import numpy as np
from jax import lax
from jax.experimental import pallas as pl
from jax.experimental.pallas import tpu as pltpu

F32 = jnp.float32
BF16 = jnp.bfloat16

HEAD_DIM = 64
LANES = 128
NORM_EPS = 1e-6
MASK_VALUE = -1e30
ROPE_THETA = 500000.0
AXIAL_THETA = 10000.0
ROT_DIMS = HEAD_DIM // 4
GRID_W = 64
DILATIONS = (1, 4, 16)
N_SIDE = 64
SSM_GROUP = 16
SSM_STATE = 64
N_SEG = 8
VMEM_LIMIT = 56 * 1024 * 1024

_ORIG = dict(aq=(0, 512), ak=(512, 512), av=(1024, 512), ag=(1536, 512), bu=(2048, 512),
             bg=(2560, 512), cq=(3072, 512), ck=(3584, 128), cv=(3712, 128), cg=(3840, 512),
             dq=(4352, 512), dk=(4864, 512), dv=(5376, 512), dg=(5888, 512))
_ORDER = ('aq', 'ak', 'dq', 'dk', 'cq', 'av', 'dv', 'bu', 'ag', 'bg', 'cg', 'dg', 'ck', 'cv')
OFF = {}
_o = 0
for _n in _ORDER:
    OFF[_n] = _o
    _o += _ORIG[_n][1]
IN_COLS = _o


def _params(*sem):
    return pltpu.CompilerParams(dimension_semantics=sem, vmem_limit_bytes=VMEM_LIMIT)


def _rms(x, w):
    return x * lax.rsqrt(jnp.mean(x * x, axis=-1, keepdims=True) + NORM_EPS) * w


def _in_proj_kernel(x_ref, nw_ref, w_ref, o_ref, h_ref):
    @pl.when(pl.program_id(1) == 0)
    def _():
        h_ref[...] = _rms(x_ref[...], nw_ref[...]).astype(BF16)

    o_ref[...] = jnp.dot(h_ref[...], w_ref[...], preferred_element_type=F32).astype(o_ref.dtype)


def in_proj(x, norm_w, w, tm=1024, tn=640):
    s, d = x.shape
    n = w.shape[1]
    tm = min(tm, s)
    return pl.pallas_call(
        _in_proj_kernel,
        grid=(s // tm, n // tn),
        in_specs=[pl.BlockSpec((tm, d), lambda i, j: (i, 0)),
                  pl.BlockSpec((1, d), lambda i, j: (0, 0)),
                  pl.BlockSpec((d, tn), lambda i, j: (0, j))],
        out_specs=pl.BlockSpec((tm, tn), lambda i, j: (i, j)),
        out_shape=jax.ShapeDtypeStruct((s, n), BF16),
        scratch_shapes=[pltpu.VMEM((tm, d), BF16)],
        compiler_params=_params("parallel", "arbitrary"),
        name="in_proj",
    )(x, norm_w.reshape(1, d), w)


def _qk_prep_kernel(cb_ref, x_ref, w_ref, g_ref, t_ref, o_ref, *, shift, mode):
    del cb_ref
    x = x_ref[...].astype(F32)
    sq = x * x
    hi = sq.astype(BF16)
    lo = (sq - hi.astype(F32)).astype(BF16)
    g = g_ref[...]
    ms = jnp.dot(hi, g, preferred_element_type=F32) + jnp.dot(lo, g, preferred_element_type=F32)
    y = x * lax.rsqrt(ms + NORM_EPS) * w_ref[...]
    y = (y * t_ref[0] + pltpu.roll(y, shift, 1) * t_ref[1]
         + pltpu.roll(y, LANES - shift, 1) * t_ref[2])
    first = lax.broadcasted_iota(jnp.int32, y.shape, 1) < HEAD_DIM
    if mode == 'q':
        o_ref[0] = jnp.where(first, y, 0.0).astype(o_ref.dtype)
        o_ref[1] = jnp.where(first, 0.0, y).astype(o_ref.dtype)
    elif mode == 'k':
        o_ref[...] = y.astype(o_ref.dtype)
    else:
        r = pltpu.roll(y, HEAD_DIM, 1)
        o_ref[0] = jnp.where(first, y, r).astype(o_ref.dtype)
        o_ref[1] = jnp.where(first, r, y).astype(o_ref.dtype)


def qk_prep(proj, col_blocks, weights, table, shift, mode, tm=1024):
    s = proj.shape[0]
    tm = min(tm, s)
    nb = len(col_blocks)
    cb = jnp.asarray(col_blocks, jnp.int32)
    gmat = jnp.asarray(np.kron(np.eye(2), np.full((HEAD_DIM, HEAD_DIM), 1.0 / HEAD_DIM)), BF16)
    if mode == 'k':
        out_shape = jax.ShapeDtypeStruct((s, nb * LANES), BF16)
        out_spec = pl.BlockSpec((tm, LANES), lambda i, j, cb: (i, j))
    else:
        out_shape = jax.ShapeDtypeStruct((2 * nb, s, LANES), BF16)
        out_spec = pl.BlockSpec((2, tm, LANES), lambda i, j, cb: (j, i, 0))
    grid_spec = pltpu.PrefetchScalarGridSpec(
        num_scalar_prefetch=1,
        grid=(s // tm, nb),
        in_specs=[pl.BlockSpec((tm, LANES), lambda i, j, cb: (i, cb[j])),
                  pl.BlockSpec((None, 1, LANES), lambda i, j, cb: (j, 0, 0)),
                  pl.BlockSpec((LANES, LANES), lambda i, j, cb: (0, 0)),
                  pl.BlockSpec((3, tm, LANES), lambda i, j, cb: (0, i, 0))],
        out_specs=out_spec,
    )
    return pl.pallas_call(
        functools.partial(_qk_prep_kernel, shift=shift, mode=mode),
        grid_spec=grid_spec,
        out_shape=out_shape,
        compiler_params=_params("parallel", "arbitrary"),
        name="qk_prep_" + mode,
    )(cb, proj, weights, gmat, table)


def _rope_tables(pos, n_dims, theta):
    inv = theta ** (-jnp.arange(0, n_dims, 2, dtype=F32) / n_dims)
    ang = pos[:, None] * inv[None, :]
    return jnp.cos(ang), jnp.sin(ang)


def _rope_table_1d(s):
    cos, sin = _rope_tables(jnp.arange(s).astype(F32), ROT_DIMS, ROPE_THETA)
    z8 = jnp.zeros_like(cos)
    rest = HEAD_DIM - ROT_DIMS
    c = jnp.concatenate([cos, cos, jnp.ones((s, rest), F32)], axis=1)
    s1 = jnp.concatenate([z8, sin, jnp.zeros((s, rest), F32)], axis=1)
    s2 = jnp.concatenate([-sin, z8, jnp.zeros((s, rest), F32)], axis=1)
    return jnp.stack([jnp.tile(t, (1, 2)) for t in (c, s1, s2)])


def _rope_table_axial(s):
    t = jnp.arange(s)
    rows = s // GRID_W
    row_c = (t // GRID_W - rows // 2).astype(F32)
    col_c = (t % GRID_W - GRID_W // 2).astype(F32)
    cr, sr = _rope_tables(row_c, HEAD_DIM // 2, AXIAL_THETA)
    cc, sc = _rope_tables(col_c, HEAD_DIM // 2, AXIAL_THETA)
    z = jnp.zeros_like(cr)
    c = jnp.concatenate([cr, cr, cc, cc], axis=1)
    s1 = jnp.concatenate([z, sr, z, sc], axis=1)
    s2 = jnp.concatenate([-sr, z, -sc, z], axis=1)
    return jnp.stack([jnp.tile(t_, (1, 2)) for t_ in (c, s1, s2)])


def _band_attn_kernel(q_ref, k_ref, v_ref, o_ref, lse_ref, *, seg_len, tq):
    n_rows = k_ref.shape[0]
    chunk = q_ref.shape[1]
    win = tq + 2 * N_SIDE
    base = pl.program_id(1) * chunk
    first = lax.broadcasted_iota(jnp.int32, (tq, LANES), 1) < HEAD_DIM
    row = lax.broadcasted_iota(jnp.int32, (2 * tq, win), 0)
    row = jnp.where(row >= tq, row - tq, row)
    col = lax.broadcasted_iota(jnp.int32, (2 * tq, win), 1)

    def tile(it, carry):
        t0 = base + it * tq
        start = pl.multiple_of(jnp.clip(t0 - N_SIDE, 0, n_rows - win), N_SIDE)
        q = q_ref[:, pl.ds(pl.multiple_of(it * tq, tq), tq), :].reshape(2 * tq, LANES)
        kw = k_ref[pl.ds(start, win), :]
        vw = v_ref[pl.ds(start, win), :]
        s = lax.dot_general(q, kw, (((1,), (1,)), ((), ())), preferred_element_type=F32)
        key = start + col
        qpos = t0 + row
        lo = (t0 // seg_len) * seg_len
        valid = (jnp.abs(key - qpos) <= N_SIDE) & (key >= lo) & (key < lo + seg_len)
        s = jnp.where(valid, s, MASK_VALUE)
        m = jnp.max(s, axis=-1, keepdims=True)
        p = jnp.exp(s - m)
        l = jnp.sum(p, axis=-1, keepdims=True)
        o = jnp.dot(p.astype(BF16), vw, preferred_element_type=F32) / l
        lse = m + jnp.log(l)
        rows = pl.ds(pl.multiple_of(it * tq, tq), tq)
        o_ref[rows, :] = jnp.where(first, o[:tq], o[tq:]).astype(o_ref.dtype)
        lse_ref[rows, :] = jnp.where(first, lse[:tq], lse[tq:])
        return carry

    lax.fori_loop(0, chunk // tq, tile, 0)


def band_attn(q_pad, k, v, v_block0, seg_len, tq=128, chunk=2048):
    nh, s, _ = q_pad.shape
    chunk = min(chunk, s)
    out = jax.ShapeDtypeStruct((s, nh * HEAD_DIM), F32)
    return pl.pallas_call(
        functools.partial(_band_attn_kernel, seg_len=seg_len, tq=tq),
        grid=(nh // 2, s // chunk),
        in_specs=[pl.BlockSpec((2, chunk, LANES), lambda m, c: (m, c, 0)),
                  pl.BlockSpec((s, LANES), lambda m, c: (0, m)),
                  pl.BlockSpec((s, LANES), lambda m, c: (0, v_block0 + m))],
        out_specs=[pl.BlockSpec((chunk, LANES), lambda m, c: (c, m)),
                   pl.BlockSpec((chunk, LANES), lambda m, c: (c, m))],
        out_shape=[out, out],
        compiler_params=_params("parallel", "arbitrary"),
        name="band_attn",
    )(q_pad, k, v)


def _band_combine_kernel(o1, o2, o3, l1, l2, l3, out_ref):
    a, b, c = l1[...], l2[...], l3[...]
    m = jnp.maximum(jnp.maximum(a, b), c)
    ea, eb, ec = jnp.exp(a - m), jnp.exp(b - m), jnp.exp(c - m)
    num = ea * o1[...] + eb * o2[...] + ec * o3[...]
    out_ref[...] = (num / (ea + eb + ec)).astype(out_ref.dtype)


def band_combine(outs, lses, tm=1024):
    s, w = outs[0].shape
    tm = min(tm, s)
    spec = pl.BlockSpec((tm, w), lambda i: (i, 0))
    return pl.pallas_call(
        _band_combine_kernel,
        grid=(s // tm,),
        in_specs=[spec] * 6,
        out_specs=spec,
        out_shape=jax.ShapeDtypeStruct((s, w), BF16),
        compiler_params=_params("parallel"),
        name="band_combine",
    )(*outs, *lses)


def _to_classes(x, d):
    if d == 1:
        return x
    *lead, s, w = x.shape
    return x.reshape(*lead, s // d, d, w).swapaxes(-3, -2).reshape(*lead, s, w)


def _from_classes(x, d):
    if d == 1:
        return x
    *lead, s, w = x.shape
    return x.reshape(*lead, d, s // d, w).swapaxes(-3, -2).reshape(*lead, s, w)


def mixer_a(q_pad, k, proj):
    s = k.shape[0]
    v = proj[:, OFF['av']:OFF['av'] + 512]
    outs, lses = [], []
    for d in DILATIONS:
        if d == 1:
            o, lse = band_attn(q_pad, k, proj, OFF['av'] // LANES, s)
        else:
            o, lse = band_attn(_to_classes(q_pad, d), _to_classes(k, d), _to_classes(v, d), 0, s // d)
        outs.append(_from_classes(o, d))
        lses.append(_from_classes(lse, d))
    return band_combine(outs, lses)


def _s5_disc_kernel(lr_ref, li_ref, ldt_ref, br_ref, bi_ref, ar_ref, ai_ref, bbr_ref, bbi_ref):
    lr, li = lr_ref[...], li_ref[...]
    dt = jnp.exp(ldt_ref[...])
    mag = jnp.exp(lr * dt)
    ar = mag * jnp.cos(li * dt)
    ai = mag * jnp.sin(li * dt)
    den = lr * lr + li * li
    cre = ((ar - 1.0) * lr + ai * li) / den
    cim = (ai * lr - (ar - 1.0) * li) / den
    br, bi = br_ref[...], bi_ref[...]
    ar_ref[...] = ar
    ai_ref[...] = ai
    bbr_ref[...] = cre * br - cim * bi
    bbi_ref[...] = cre * bi + cim * br


def s5_discretise(lam_re, lam_im, log_dt, b_re, b_im):
    two, g, p = lam_re.shape
    c = b_re.shape[-1]
    rep = lambda t: jnp.repeat(t.reshape(two * g, p), c, axis=1)
    shp = jax.ShapeDtypeStruct((two * g, p * c), F32)
    ar, ai, bbr, bbi = pl.pallas_call(
        _s5_disc_kernel,
        out_shape=[shp] * 4,
        name="s5_disc",
    )(rep(lam_re), rep(lam_im), log_dt.reshape(two * g, 1),
      b_re.reshape(two * g, p * c), b_im.reshape(two * g, p * c))
    unrep = lambda t: t.reshape(two, g, p, c)[..., 0]
    return unrep(ar), unrep(ai), bbr.reshape(two, g, p, c), bbi.reshape(two, g, p, c)


def _block_diag(m):
    g, r, c = m.shape
    eye = jnp.eye(g, dtype=m.dtype)
    return jnp.einsum('grc,gh->grhc', m, eye).reshape(g * r, g * c)


def _s5_scan_kernel(u_ref, bre_ref, bim_ref, ar_ref, ai_ref, x0r_ref, x0i_ref, *rest,
                    emit, lane_blk):
    if emit:
        cre_ref, cim_ref, y_ref, xr_ref, xi_ref, sr_ref, si_ref = rest
    else:
        er_ref, ei_ref, xr_ref, xi_ref, sr_ref, si_ref = rest
    ic = pl.program_id(1)
    n_i = u_ref.shape[0] // N_SEG
    n_state = xr_ref.shape[1]

    @pl.when(ic == 0)
    def _():
        sr_ref[...] = x0r_ref[...]
        si_ref[...] = x0i_ref[...]

    u = u_ref[...]
    xr_ref[...] = jnp.dot(u, bre_ref[...], preferred_element_type=F32)
    xi_ref[...] = jnp.dot(u, bim_ref[...], preferred_element_type=F32)

    for lb in range(n_state // lane_blk):
        cols = pl.ds(lb * lane_blk, lane_blk)
        ar = jnp.broadcast_to(ar_ref[:, cols], (N_SEG, lane_blk))
        ai = jnp.broadcast_to(ai_ref[:, cols], (N_SEG, lane_blk))

        def step(i, carry):
            xr, xi = carry
            rows = pl.ds(pl.multiple_of(i * N_SEG, N_SEG), N_SEG)
            nr = ar * xr - ai * xi + xr_ref[rows, cols]
            ni = ar * xi + ai * xr + xi_ref[rows, cols]
            xr_ref[rows, cols] = nr
            xi_ref[rows, cols] = ni
            return nr, ni

        fr, fi = lax.fori_loop(0, n_i, step, (sr_ref[:, cols], si_ref[:, cols]), unroll=8)
        sr_ref[:, cols] = fr
        si_ref[:, cols] = fi

    if emit:
        y_ref[...] = (jnp.dot(xr_ref[...].astype(BF16), cre_ref[...], preferred_element_type=F32)
                      - jnp.dot(xi_ref[...].astype(BF16), cim_ref[...], preferred_element_type=F32))
    else:
        @pl.when(ic == pl.num_programs(1) - 1)
        def _():
            er_ref[...] = sr_ref[...]
            ei_ref[...] = si_ref[...]


def s5_scan(u8, bmat_re, bmat_im, a_re, a_im, x0_re, x0_im, cmat_re=None, cmat_im=None,
            rows=256, lane_blk=512):
    two, s, w = u8.shape
    n_state = bmat_re.shape[-1]
    rows = min(rows, s)
    emit = cmat_re is not None
    dspec = lambda shape: pl.BlockSpec((None,) + shape, lambda d, i: (d,) + (0,) * len(shape))
    in_specs = [pl.BlockSpec((None, rows, w), lambda d, i: (d, i, 0)),
                dspec((w, n_state)), dspec((w, n_state)),
                dspec((1, n_state)), dspec((1, n_state)),
                dspec((N_SEG, n_state)), dspec((N_SEG, n_state))]
    args = [u8, bmat_re, bmat_im, a_re, a_im, x0_re, x0_im]
    if emit:
        in_specs += [dspec((n_state, w)), dspec((n_state, w))]
        args += [cmat_re, cmat_im]
        out_specs = pl.BlockSpec((None, rows, w), lambda d, i: (d, i, 0))
        out_shape = jax.ShapeDtypeStruct((two, s, w), F32)
    else:
        out_specs = [dspec((N_SEG, n_state))] * 2
        out_shape = [jax.ShapeDtypeStruct((two, N_SEG, n_state), F32)] * 2
    return pl.pallas_call(
        functools.partial(_s5_scan_kernel, emit=emit, lane_blk=lane_blk),
        grid=(two, s // rows),
        in_specs=in_specs,
        out_specs=out_specs,
        out_shape=out_shape,
        scratch_shapes=[pltpu.VMEM((rows, n_state), F32), pltpu.VMEM((rows, n_state), F32),
                        pltpu.VMEM((N_SEG, n_state), F32), pltpu.VMEM((N_SEG, n_state), F32)],
        compiler_params=_params("parallel", "arbitrary"),
        name="s5_scan_emit" if emit else "s5_scan_ends",
    )(*args)


def _s5_carry_kernel(er_ref, ei_ref, ar_ref, ai_ref, cr_ref, ci_ref, *, seg_len):
    pr, pi = ar_ref[...], ai_ref[...]
    n = seg_len
    assert n & (n - 1) == 0
    while n > 1:
        pr, pi = pr * pr - pi * pi, 2.0 * pr * pi
        n //= 2
    cr = jnp.zeros_like(pr)
    ci = jnp.zeros_like(pi)
    cr_ref[0:1, :] = cr
    ci_ref[0:1, :] = ci
    for j in range(1, N_SEG):
        cr, ci = (er_ref[j - 1:j, :] + pr * cr - pi * ci, ei_ref[j - 1:j, :] + pr * ci + pi * cr)
        cr_ref[j:j + 1, :] = cr
        ci_ref[j:j + 1, :] = ci


def s5_carry(e_re, e_im, a_re, a_im, seg_len):
    two, _, n_state = e_re.shape
    dspec = lambda r: pl.BlockSpec((None, r, n_state), lambda d: (d, 0, 0))
    shp = jax.ShapeDtypeStruct((two, N_SEG, n_state), F32)
    return pl.pallas_call(
        functools.partial(_s5_carry_kernel, seg_len=seg_len),
        grid=(two,),
        in_specs=[dspec(N_SEG), dspec(N_SEG), dspec(1), dspec(1)],
        out_specs=[dspec(N_SEG), dspec(N_SEG)],
        out_shape=[shp, shp],
        compiler_params=_params("parallel"),
        name="s5_carry",
    )(e_re, e_im, a_re, a_im)


def _s5_glu_kernel(u_ref, yf_ref, yb_ref, d_ref, w_ref, b_ref, o_ref):
    width = o_ref.shape[1]
    y = d_ref[...] * u_ref[...].astype(F32) + yf_ref[...] + yb_ref[...]
    c = math.sqrt(2.0 / math.pi)
    y = 0.5 * y * (1.0 + jnp.tanh(c * (y + 0.044715 * (y * y * y))))
    z = jnp.dot(y.astype(BF16), w_ref[...], preferred_element_type=F32) + b_ref[...]
    o_ref[...] = (z[:, :width] * jax.nn.sigmoid(z[:, width:])).astype(o_ref.dtype)


def s5_glu(proj, yf, yb, d_skip, w_glu, b_glu, tm=1024):
    s, w = yf.shape
    tm = min(tm, s)
    ub = OFF['bu'] // w
    return pl.pallas_call(
        _s5_glu_kernel,
        grid=(s // tm,),
        in_specs=[pl.BlockSpec((tm, w), lambda i: (i, ub)),
                  pl.BlockSpec((tm, w), lambda i: (i, 0)),
                  pl.BlockSpec((tm, w), lambda i: (i, 0)),
                  pl.BlockSpec((1, w), lambda i: (0, 0)),
                  pl.BlockSpec((w, 2 * w), lambda i: (0, 0)),
                  pl.BlockSpec((1, 2 * w), lambda i: (0, 0))],
        out_specs=pl.BlockSpec((tm, w), lambda i: (i, 0)),
        out_shape=jax.ShapeDtypeStruct((s, w), BF16),
        compiler_params=_params("parallel"),
        name="s5_glu",
    )(proj, yf, yb, d_skip.reshape(1, w), w_glu, b_glu.reshape(1, 2 * w))


def _to_segments(x):
    *lead, s, w = x.shape
    return x.reshape(*lead, N_SEG, s // N_SEG, w).swapaxes(-3, -2).reshape(*lead, s, w)


def _from_segments(x):
    *lead, s, w = x.shape
    return x.reshape(*lead, s // N_SEG, N_SEG, w).swapaxes(-3, -2).reshape(*lead, s, w)


def mixer_b(proj, lam_re, lam_im, log_dt, b_re, b_im, c_re, c_im, d_skip, w_glu, b_glu):
    s = proj.shape[0]
    two, g, p = lam_re.shape
    a_re, a_im, bb_re, bb_im = s5_discretise(lam_re, lam_im, log_dt, b_re, b_im)
    bmat_re = jnp.stack([_block_diag(bb_re[d].swapaxes(1, 2)) for d in range(two)]).astype(BF16)
    bmat_im = jnp.stack([_block_diag(bb_im[d].swapaxes(1, 2)) for d in range(two)]).astype(BF16)
    cmat_re = jnp.stack([_block_diag(c_re[d].swapaxes(1, 2)) for d in range(two)]).astype(BF16)
    cmat_im = jnp.stack([_block_diag(c_im[d].swapaxes(1, 2)) for d in range(two)]).astype(BF16)
    a_re = a_re.reshape(two, 1, g * p)
    a_im = a_im.reshape(two, 1, g * p)
    u = proj[:, OFF['bu']:OFF['bu'] + 512]
    u8 = _to_segments(jnp.stack([u, jnp.flip(u, axis=0)]))
    zeros = jnp.zeros((two, N_SEG, g * p), F32)
    e_re, e_im = s5_scan(u8, bmat_re, bmat_im, a_re, a_im, zeros, zeros)
    x0_re, x0_im = s5_carry(e_re, e_im, a_re, a_im, s // N_SEG)
    y8 = s5_scan(u8, bmat_re, bmat_im, a_re, a_im, x0_re, x0_im, cmat_re, cmat_im)
    y = _from_segments(y8)
    return s5_glu(proj, y[0], jnp.flip(y[1], axis=0), d_skip, w_glu.astype(BF16), b_glu)


def _flash_step(q, k_ref, v_ref, kc, tk, m_ref, l_ref, acc_ref):
    rows = pl.ds(pl.multiple_of(kc * tk, tk), tk)
    k = k_ref[rows, :]
    v = v_ref[rows, :]
    s = lax.dot_general(q, k, (((1,), (1,)), ((), ())), preferred_element_type=F32)
    m_prev = m_ref[...]
    m_new = jnp.maximum(m_prev, jnp.max(s, axis=-1, keepdims=True))
    alpha = jnp.exp(m_prev - m_new)
    p = jnp.exp(s - m_new)
    l_ref[...] = alpha * l_ref[...] + jnp.sum(p, axis=-1, keepdims=True)
    acc_ref[...] = alpha * acc_ref[...] + jnp.dot(p.astype(BF16), v, preferred_element_type=F32)
    m_ref[...] = m_new


def _flash(q, k_ref, v_ref, tk, m_ref, l_ref, acc_ref):
    m_ref[...] = jnp.full(m_ref.shape, -jnp.inf, F32)
    l_ref[...] = jnp.zeros(l_ref.shape, F32)
    acc_ref[...] = jnp.zeros(acc_ref.shape, F32)

    def body(kc, carry):
        _flash_step(q, k_ref, v_ref, kc, tk, m_ref, l_ref, acc_ref)
        return carry

    lax.fori_loop(0, k_ref.shape[0] // tk, body, 0)
    return acc_ref[...] / l_ref[...]


def _gqa_kernel(q_ref, k_ref, v_ref, o_ref, m_ref, l_ref, acc_ref, *, tk):
    nh, tq, _ = q_ref.shape
    q = q_ref[...].reshape(nh * tq, LANES)
    o = _flash(q, k_ref, v_ref, tk, m_ref, l_ref, acc_ref)
    first = lax.broadcasted_iota(jnp.int32, (tq, LANES), 1) < HEAD_DIM
    for pair in range(nh // 2):
        even = o[(2 * pair) * tq:(2 * pair + 1) * tq]
        odd = o[(2 * pair + 1) * tq:(2 * pair + 2) * tq]
        o_ref[:, pair * LANES:(pair + 1) * LANES] = jnp.where(first, even, odd).astype(o_ref.dtype)


def gqa_attn(q_pad, k_dup, v_dup, tq=128, tk=512):
    nh, s, _ = q_pad.shape
    nkv = k_dup.shape[0]
    grp = nh // nkv
    tq, tk = min(tq, s), min(tk, s)
    r = grp * tq
    return pl.pallas_call(
        functools.partial(_gqa_kernel, tk=tk),
        grid=(nkv, s // tq),
        in_specs=[pl.BlockSpec((grp, tq, LANES), lambda g, i: (g, i, 0)),
                  pl.BlockSpec((None, s, LANES), lambda g, i: (g, 0, 0)),
                  pl.BlockSpec((None, s, LANES), lambda g, i: (g, 0, 0))],
        out_specs=pl.BlockSpec((tq, grp * HEAD_DIM), lambda g, i: (i, g)),
        out_shape=jax.ShapeDtypeStruct((s, nh * HEAD_DIM), BF16),
        scratch_shapes=[pltpu.VMEM((r, 1), F32), pltpu.VMEM((r, 1), F32), pltpu.VMEM((r, LANES), F32)],
        compiler_params=_params("parallel", "arbitrary"),
        name="gqa_attn",
    )(q_pad, k_dup, v_dup)


def _diff_kernel(q_ref, k_ref, v_ref, lam_ref, w_ref, o_ref, m_ref, l_ref, acc_ref, *, tk):
    _, tq, _ = q_ref.shape
    q = q_ref[...].reshape(2 * tq, LANES)
    o = _flash(q, k_ref, v_ref, tk, m_ref, l_ref, acc_ref)
    lp = lam_ref[...]
    lam_init = lp[4:5, 0:1]
    lam = (jnp.exp(jnp.sum(lp[0:1] * lp[1:2], axis=-1, keepdims=True))
           - jnp.exp(jnp.sum(lp[2:3] * lp[3:4], axis=-1, keepdims=True)) + lam_init)
    d = o[:tq] - lam * o[tq:]
    o_ref[...] = (_rms(d, w_ref[...]) * (1.0 - lam_init)).astype(o_ref.dtype)


def diff_attn(q_pad, k, proj, lam_rows, subln_w, tq=256, tk=512):
    n2, s, _ = q_pad.shape
    nh = n2 // 2
    tq, tk = min(tq, s), min(tk, s)
    vb = OFF['dv'] // LANES
    return pl.pallas_call(
        functools.partial(_diff_kernel, tk=tk),
        grid=(nh, s // tq),
        in_specs=[pl.BlockSpec((2, tq, LANES), lambda h, i: (h, i, 0)),
                  pl.BlockSpec((s, LANES), lambda h, i: (0, h)),
                  pl.BlockSpec((s, LANES), lambda h, i: (0, vb + h)),
                  pl.BlockSpec((8, LANES), lambda h, i: (0, 0)),
                  pl.BlockSpec((1, LANES), lambda h, i: (0, 0))],
        out_specs=pl.BlockSpec((tq, LANES), lambda h, i: (i, h)),
        out_shape=jax.ShapeDtypeStruct((s, nh * LANES), BF16),
        scratch_shapes=[pltpu.VMEM((2 * tq, 1), F32), pltpu.VMEM((2 * tq, 1), F32),
                        pltpu.VMEM((2 * tq, LANES), F32)],
        compiler_params=_params("parallel", "arbitrary"),
        name="diff_attn",
    )(q_pad, k, proj, lam_rows, subln_w.reshape(1, LANES))


def _out_ple_kernel(x_ref, a_ref, b_ref, c_ref, d_ref, g_ref, p_ref, wo_ref, nw_ref, gw_ref, pw_ref,
                    o_ref):
    g = g_ref[...].astype(F32)
    mix = jnp.concatenate([a_ref[...], b_ref[...], c_ref[...], d_ref[...]], axis=-1).astype(F32)
    mixed = (mix * (g * jax.nn.sigmoid(g))).astype(BF16)
    x1 = x_ref[...] + jnp.dot(mixed, wo_ref[...], preferred_element_type=F32)
    h = _rms(x1, nw_ref[...]).astype(BF16)
    gate = jax.nn.sigmoid(jnp.dot(h, gw_ref[...], preferred_element_type=F32))
    pe = jnp.dot(p_ref[...].astype(BF16), pw_ref[...], preferred_element_type=F32)
    o_ref[...] = x1 + gate * pe


def out_ple(x, a, b, c, d, proj, p, w_out, ple_norm_w, ple_gate_w, ple_w, tm=256):
    s, dm = x.shape
    bw = a.shape[1]
    pd = p.shape[1]
    tm = min(tm, s)
    gb = OFF['ag'] // dm
    row = lambda w: pl.BlockSpec((tm, w), lambda i: (i, 0))
    const = lambda r, c_: pl.BlockSpec((r, c_), lambda i: (0, 0), pipeline_mode=pl.Buffered(1))
    return pl.pallas_call(
        _out_ple_kernel,
        grid=(s // tm,),
        in_specs=[row(dm), row(bw), row(bw), row(bw), row(bw),
                  pl.BlockSpec((tm, dm), lambda i: (i, gb)), row(pd),
                  const(dm, dm), const(1, dm), const(dm, dm), const(pd, dm)],
        out_specs=row(dm),
        out_shape=jax.ShapeDtypeStruct((s, dm), F32),
        compiler_params=_params("parallel"),
        name="out_ple",
    )(x, a, b, c, d, proj, p, w_out, ple_norm_w.reshape(1, dm), ple_gate_w, ple_w)


def _reorder_cols(w):
    return jnp.concatenate([w[:, _ORIG[n][0]:_ORIG[n][0] + _ORIG[n][1]] for n in _ORDER], axis=1)


def _pair_weights(ws, scale):
    return jnp.stack([jnp.tile(w.astype(F32) * scale, 2).reshape(1, LANES) for w in ws])


def kernel(x, p, norm_w, w_in, w_out, a_q_norm, a_k_norm, s5_lambda_re, s5_lambda_im, s5_log_dt, s5_b_re, s5_b_im, s5_c_re, s5_c_im, s5_d, s5_w_glu, s5_b_glu, c_q_norm, c_k_norm, d_q_norm, d_k_norm, d_lambda_q1, d_lambda_k1, d_lambda_q2, d_lambda_k2, d_subln, ple_norm_w, ple_gate_w, ple_w):
    bsz, s, dm = x.shape
    assert bsz == 1
    depth = w_in.shape[0]
    scale = HEAD_DIM ** -0.5
    tab_1d = _rope_table_1d(s)
    tab_ax = _rope_table_axial(s)
    blk = lambda name, n: tuple(OFF[name] // LANES + j for j in range(n))
    xs = x[0]
    for i in range(depth):
        proj = in_proj(xs, norm_w[i], _reorder_cols(w_in[i].astype(BF16)))

        q0 = qk_prep(proj, blk('aq', 4) + blk('dq', 4),
                     _pair_weights([a_q_norm[i]] * 4 + [d_q_norm[i]] * 4, scale), tab_1d, ROT_DIMS // 2, 'q')
        k0 = qk_prep(proj, blk('ak', 4) + blk('dk', 4),
                     _pair_weights([a_k_norm[i]] * 4 + [d_k_norm[i]] * 4, 1.0), tab_1d, ROT_DIMS // 2, 'k')
        q1 = qk_prep(proj, blk('cq', 4), _pair_weights([c_q_norm[i]] * 4, scale), tab_ax,
                     HEAD_DIM // 4, 'q')
        k1 = qk_prep(proj, blk('ck', 1), _pair_weights([c_k_norm[i]], 1.0), tab_ax, HEAD_DIM // 4, 'kdup')

        a_out = mixer_a(q0[:8], k0[:, :512], proj)

        b_out = mixer_b(proj, s5_lambda_re[i], s5_lambda_im[i], s5_log_dt[i], s5_b_re[i], s5_b_im[i],
                        s5_c_re[i], s5_c_im[i], s5_d[i], s5_w_glu[i], s5_b_glu[i])

        cv = proj[:, OFF['cv']:OFF['cv'] + LANES]
        v_dup = jnp.stack([jnp.tile(cv[:, :HEAD_DIM], (1, 2)), jnp.tile(cv[:, HEAD_DIM:], (1, 2))])
        c_out = gqa_attn(q1, k1, v_dup)

        lam_init = 0.8 - 0.6 * math.exp(-0.3 * i)
        pad = lambda v: jnp.pad(v.astype(F32), (0, LANES - HEAD_DIM))
        lam_rows = jnp.stack([pad(d_lambda_q1[i]), pad(d_lambda_k1[i]), pad(d_lambda_q2[i]),
                              pad(d_lambda_k2[i]), jnp.full((LANES,), lam_init, F32)]
                             + [jnp.zeros((LANES,), F32)] * 3)
        d_out = diff_attn(q0[8:], k0[:, 512:], proj, lam_rows, d_subln[i])

        xs = out_ple(xs, a_out, b_out, c_out, d_out, proj, p[i, 0], w_out[i].astype(BF16),
                     ple_norm_w[i], ple_gate_w[i].astype(BF16), ple_w[i].astype(BF16))
    return xs[None]
```

```python
import functools
import math

import jax
import jax.numpy as jnp
import numpy as np
from jax import lax
from jax.experimental import pallas as pl
from jax.experimental.pallas import tpu as pltpu

F32 = jnp.float32
BF16 = jnp.bfloat16

HEAD_DIM = 64
LANES = 128
NORM_EPS = 1e-6
MASK_VALUE = -1e30
ROPE_THETA = 500000.0
AXIAL_THETA = 10000.0
ROT_DIMS = HEAD_DIM // 4
GRID_W = 64
DILATIONS = (1, 4, 16)
N_SIDE = 64
SSM_GROUP = 16
SSM_STATE = 64
N_SEG = 8
FLASH_ROWS = 64
VMEM_LIMIT = 56 * 1024 * 1024

_ORIG = dict(aq=(0, 512), ak=(512, 512), av=(1024, 512), ag=(1536, 512), bu=(2048, 512),
             bg=(2560, 512), cq=(3072, 512), ck=(3584, 128), cv=(3712, 128), cg=(3840, 512),
             dq=(4352, 512), dk=(4864, 512), dv=(5376, 512), dg=(5888, 512))
_ORDER = ('aq', 'ak', 'dq', 'dk', 'cq', 'av', 'dv', 'bu', 'ag', 'bg', 'cg', 'dg', 'ck', 'cv')
OFF = {}
_o = 0
for _n in _ORDER:
    OFF[_n] = _o
    _o += _ORIG[_n][1]
IN_COLS = _o


def _params(*sem):
    return pltpu.CompilerParams(dimension_semantics=sem, vmem_limit_bytes=VMEM_LIMIT)


def _rms(x, w):
    return x * lax.rsqrt(jnp.mean(x * x, axis=-1, keepdims=True) + NORM_EPS) * w


def _in_proj_kernel(x_ref, nw_ref, w_ref, o_ref, h_ref):
    @pl.when(pl.program_id(1) == 0)
    def _():
        h_ref[...] = _rms(x_ref[...], nw_ref[...]).astype(BF16)

    o_ref[...] = jnp.dot(h_ref[...], w_ref[...], preferred_element_type=F32).astype(o_ref.dtype)


def in_proj(x, norm_w, w, tm=1024, tn=640):
    s, d = x.shape
    n = w.shape[1]
    tm = min(tm, s)
    return pl.pallas_call(
        _in_proj_kernel,
        grid=(s // tm, n // tn),
        in_specs=[pl.BlockSpec((tm, d), lambda i, j: (i, 0)),
                  pl.BlockSpec((1, d), lambda i, j: (0, 0)),
                  pl.BlockSpec((d, tn), lambda i, j: (0, j))],
        out_specs=pl.BlockSpec((tm, tn), lambda i, j: (i, j)),
        out_shape=jax.ShapeDtypeStruct((s, n), BF16),
        scratch_shapes=[pltpu.VMEM((tm, d), BF16)],
        compiler_params=_params("parallel", "arbitrary"),
        name="in_proj",
    )(x, norm_w.reshape(1, d), w)


def _qk_prep_kernel(cb_ref, x_ref, w_ref, g_ref, t_ref, o_ref, *, shift, mode):
    del cb_ref
    x = x_ref[...].astype(F32)
    sq = x * x
    hi = sq.astype(BF16)
    lo = (sq - hi.astype(F32)).astype(BF16)
    g = g_ref[...]
    ms = jnp.dot(hi, g, preferred_element_type=F32) + jnp.dot(lo, g, preferred_element_type=F32)
    y = x * lax.rsqrt(ms + NORM_EPS) * w_ref[...]
    y = (y * t_ref[0] + pltpu.roll(y, shift, 1) * t_ref[1]
         + pltpu.roll(y, LANES - shift, 1) * t_ref[2])
    first = lax.broadcasted_iota(jnp.int32, y.shape, 1) < HEAD_DIM
    if mode == 'q':
        o_ref[0] = jnp.where(first, y, 0.0).astype(o_ref.dtype)
        o_ref[1] = jnp.where(first, 0.0, y).astype(o_ref.dtype)
    elif mode == 'k':
        o_ref[...] = y.astype(o_ref.dtype)
    else:
        r = pltpu.roll(y, HEAD_DIM, 1)
        o_ref[0] = jnp.where(first, y, r).astype(o_ref.dtype)
        o_ref[1] = jnp.where(first, r, y).astype(o_ref.dtype)


def qk_prep(proj, col_blocks, weights, table, shift, mode, tm=1024):
    s = proj.shape[0]
    tm = min(tm, s)
    nb = len(col_blocks)
    cb = jnp.asarray(col_blocks, jnp.int32)
    gmat = jnp.asarray(np.kron(np.eye(2), np.full((HEAD_DIM, HEAD_DIM), 1.0 / HEAD_DIM)), BF16)
    if mode == 'k':
        out_shape = jax.ShapeDtypeStruct((s, nb * LANES), BF16)
        out_spec = pl.BlockSpec((tm, LANES), lambda i, j, cb: (i, j))
    else:
        out_shape = jax.ShapeDtypeStruct((2 * nb, s, LANES), BF16)
        out_spec = pl.BlockSpec((2, tm, LANES), lambda i, j, cb: (j, i, 0))
    grid_spec = pltpu.PrefetchScalarGridSpec(
        num_scalar_prefetch=1,
        grid=(s // tm, nb),
        in_specs=[pl.BlockSpec((tm, LANES), lambda i, j, cb: (i, cb[j])),
                  pl.BlockSpec((None, 1, LANES), lambda i, j, cb: (j, 0, 0)),
                  pl.BlockSpec((LANES, LANES), lambda i, j, cb: (0, 0)),
                  pl.BlockSpec((3, tm, LANES), lambda i, j, cb: (0, i, 0))],
        out_specs=out_spec,
    )
    return pl.pallas_call(
        functools.partial(_qk_prep_kernel, shift=shift, mode=mode),
        grid_spec=grid_spec,
        out_shape=out_shape,
        compiler_params=_params("parallel", "arbitrary"),
        name="qk_prep_" + mode,
    )(cb, proj, weights, gmat, table)


def _rope_tables(pos, n_dims, theta):
    inv = theta ** (-jnp.arange(0, n_dims, 2, dtype=F32) / n_dims)
    ang = pos[:, None] * inv[None, :]
    return jnp.cos(ang), jnp.sin(ang)


def _rope_table_1d(s):
    cos, sin = _rope_tables(jnp.arange(s).astype(F32), ROT_DIMS, ROPE_THETA)
    z8 = jnp.zeros_like(cos)
    rest = HEAD_DIM - ROT_DIMS
    c = jnp.concatenate([cos, cos, jnp.ones((s, rest), F32)], axis=1)
    s1 = jnp.concatenate([z8, sin, jnp.zeros((s, rest), F32)], axis=1)
    s2 = jnp.concatenate([-sin, z8, jnp.zeros((s, rest), F32)], axis=1)
    return jnp.stack([jnp.tile(t, (1, 2)) for t in (c, s1, s2)])


def _rope_table_axial(s):
    t = jnp.arange(s)
    rows = s // GRID_W
    row_c = (t // GRID_W - rows // 2).astype(F32)
    col_c = (t % GRID_W - GRID_W // 2).astype(F32)
    cr, sr = _rope_tables(row_c, HEAD_DIM // 2, AXIAL_THETA)
    cc, sc = _rope_tables(col_c, HEAD_DIM // 2, AXIAL_THETA)
    z = jnp.zeros_like(cr)
    c = jnp.concatenate([cr, cr, cc, cc], axis=1)
    s1 = jnp.concatenate([z, sr, z, sc], axis=1)
    s2 = jnp.concatenate([-sr, z, -sc, z], axis=1)
    return jnp.stack([jnp.tile(t_, (1, 2)) for t_ in (c, s1, s2)])


def _band_attn_kernel(q_ref, k_ref, v_ref, o_ref, lse_ref, *, seg_len, tq):
    n_rows = k_ref.shape[0]
    chunk = q_ref.shape[1]
    win = tq + 2 * N_SIDE
    base = pl.program_id(1) * chunk
    first = lax.broadcasted_iota(jnp.int32, (tq, LANES), 1) < HEAD_DIM
    row = lax.broadcasted_iota(jnp.int32, (2 * tq, win), 0)
    row = jnp.where(row >= tq, row - tq, row)
    col = lax.broadcasted_iota(jnp.int32, (2 * tq, win), 1)

    def tile(it, carry):
        t0 = base + it * tq
        start = pl.multiple_of(jnp.clip(t0 - N_SIDE, 0, n_rows - win), N_SIDE)
        q = q_ref[:, pl.ds(pl.multiple_of(it * tq, tq), tq), :].reshape(2 * tq, LANES)
        kw = k_ref[pl.ds(start, win), :]
        vw = v_ref[pl.ds(start, win), :]
        s = lax.dot_general(q, kw, (((1,), (1,)), ((), ())), preferred_element_type=F32)
        key = start + col
        qpos = t0 + row
        lo = (t0 // seg_len) * seg_len
        valid = (jnp.abs(key - qpos) <= N_SIDE) & (key >= lo) & (key < lo + seg_len)
        s = jnp.where(valid, s, MASK_VALUE)
        m = jnp.max(s, axis=-1, keepdims=True)
        p = jnp.exp(s - m)
        l = jnp.sum(p, axis=-1, keepdims=True)
        o = jnp.dot(p.astype(BF16), vw, preferred_element_type=F32) / l
        lse = m + jnp.log(l)
        rows = pl.ds(pl.multiple_of(it * tq, tq), tq)
        o_ref[rows, :] = jnp.where(first, o[:tq], o[tq:]).astype(o_ref.dtype)
        lse_ref[rows, :] = jnp.where(first, lse[:tq], lse[tq:])
        return carry

    lax.fori_loop(0, chunk // tq, tile, 0, unroll=4)


def band_attn(q_pad, k, v, v_block0, seg_len, tq=128, chunk=2048):
    nh, s, _ = q_pad.shape
    chunk = min(chunk, s)
    out = jax.ShapeDtypeStruct((s, nh * HEAD_DIM), F32)
    return pl.pallas_call(
        functools.partial(_band_attn_kernel, seg_len=seg_len, tq=tq),
        grid=(nh // 2, s // chunk),
        in_specs=[pl.BlockSpec((2, chunk, LANES), lambda m, c: (m, c, 0)),
                  pl.BlockSpec((s, LANES), lambda m, c: (0, m)),
                  pl.BlockSpec((s, LANES), lambda m, c: (0, v_block0 + m))],
        out_specs=[pl.BlockSpec((chunk, LANES), lambda m, c: (c, m)),
                   pl.BlockSpec((chunk, LANES), lambda m, c: (c, m))],
        out_shape=[out, out],
        compiler_params=_params("parallel", "arbitrary"),
        name="band_attn",
    )(q_pad, k, v)


def _band_combine_kernel(o1, o2, o3, l1, l2, l3, out_ref):
    a, b, c = l1[...], l2[...], l3[...]
    m = jnp.maximum(jnp.maximum(a, b), c)
    ea, eb, ec = jnp.exp(a - m), jnp.exp(b - m), jnp.exp(c - m)
    num = ea * o1[...] + eb * o2[...] + ec * o3[...]
    out_ref[...] = (num / (ea + eb + ec)).astype(out_ref.dtype)


def band_combine(outs, lses, tm=1024):
    s, w = outs[0].shape
    tm = min(tm, s)
    spec = pl.BlockSpec((tm, w), lambda i: (i, 0))
    return pl.pallas_call(
        _band_combine_kernel,
        grid=(s // tm,),
        in_specs=[spec] * 6,
        out_specs=spec,
        out_shape=jax.ShapeDtypeStruct((s, w), BF16),
        compiler_params=_params("parallel"),
        name="band_combine",
    )(*outs, *lses)


def _to_classes(x, d):
    if d == 1:
        return x
    *lead, s, w = x.shape
    return x.reshape(*lead, s // d, d, w).swapaxes(-3, -2).reshape(*lead, s, w)


def _from_classes(x, d):
    if d == 1:
        return x
    *lead, s, w = x.shape
    return x.reshape(*lead, d, s // d, w).swapaxes(-3, -2).reshape(*lead, s, w)


def mixer_a(q_pad, k, proj):
    s = k.shape[0]
    v = proj[:, OFF['av']:OFF['av'] + 512]
    outs, lses = [], []
    for d in DILATIONS:
        if d == 1:
            o, lse = band_attn(q_pad, k, proj, OFF['av'] // LANES, s)
        else:
            o, lse = band_attn(_to_classes(q_pad, d), _to_classes(k, d), _to_classes(v, d), 0, s // d)
        outs.append(_from_classes(o, d))
        lses.append(_from_classes(lse, d))
    return band_combine(outs, lses)


def _s5_disc_kernel(lr_ref, li_ref, ldt_ref, br_ref, bi_ref, ar_ref, ai_ref, bbr_ref, bbi_ref):
    lr, li = lr_ref[...], li_ref[...]
    dt = jnp.exp(ldt_ref[...])
    mag = jnp.exp(lr * dt)
    ar = mag * jnp.cos(li * dt)
    ai = mag * jnp.sin(li * dt)
    den = lr * lr + li * li
    cre = ((ar - 1.0) * lr + ai * li) / den
    cim = (ai * lr - (ar - 1.0) * li) / den
    br, bi = br_ref[...], bi_ref[...]
    ar_ref[...] = ar
    ai_ref[...] = ai
    bbr_ref[...] = cre * br - cim * bi
    bbi_ref[...] = cre * bi + cim * br


def s5_discretise(lam_re, lam_im, log_dt, b_re, b_im):
    two, g, p = lam_re.shape
    c = b_re.shape[-1]
    rep = lambda t: jnp.repeat(t.reshape(two * g, p), c, axis=1)
    shp = jax.ShapeDtypeStruct((two * g, p * c), F32)
    ar, ai, bbr, bbi = pl.pallas_call(
        _s5_disc_kernel,
        out_shape=[shp] * 4,
        name="s5_disc",
    )(rep(lam_re), rep(lam_im), log_dt.reshape(two * g, 1),
      b_re.reshape(two * g, p * c), b_im.reshape(two * g, p * c))
    unrep = lambda t: t.reshape(two, g, p, c)[..., 0]
    return unrep(ar), unrep(ai), bbr.reshape(two, g, p, c), bbi.reshape(two, g, p, c)


def _block_diag(m):
    g, r, c = m.shape
    eye = jnp.eye(g, dtype=m.dtype)
    return jnp.einsum('grc,gh->grhc', m, eye).reshape(g * r, g * c)


def _s5_scan_kernel(u_ref, bre_ref, bim_ref, ar_ref, ai_ref, x0r_ref, x0i_ref, *rest,
                    emit, lane_blk):
    if emit:
        cre_ref, cim_ref, y_ref, xr_ref, xi_ref, sr_ref, si_ref = rest
    else:
        er_ref, ei_ref, xr_ref, xi_ref, sr_ref, si_ref = rest
    back = pl.program_id(0)
    ic = pl.program_id(1)
    n_i = u_ref.shape[0] // N_SEG
    n_state = xr_ref.shape[1]

    @pl.when(ic == 0)
    def _():
        sr_ref[...] = x0r_ref[...]
        si_ref[...] = x0i_ref[...]

    w = u_ref.shape[1]
    halves = [(slice(h * w // 2, (h + 1) * w // 2), slice(h * n_state // 2, (h + 1) * n_state // 2))
              for h in range(2)]
    for ch, st in halves:
        u = u_ref[:, ch]
        xr_ref[:, st] = jnp.dot(u, bre_ref[ch, st], preferred_element_type=F32)
        xi_ref[:, st] = jnp.dot(u, bim_ref[ch, st], preferred_element_type=F32)

    for lb in range(n_state // lane_blk):
        cols = pl.ds(lb * lane_blk, lane_blk)
        ar = jnp.broadcast_to(ar_ref[:, cols], (N_SEG, lane_blk))
        ai = jnp.broadcast_to(ai_ref[:, cols], (N_SEG, lane_blk))

        def step(i, carry):
            xr, xi = carry
            i = i + back * (n_i - 1 - 2 * i)
            rows = pl.ds(pl.multiple_of(i * N_SEG, N_SEG), N_SEG)
            nr = ar * xr - ai * xi + xr_ref[rows, cols]
            ni = ar * xi + ai * xr + xi_ref[rows, cols]
            xr_ref[rows, cols] = nr
            xi_ref[rows, cols] = ni
            return nr, ni

        fr, fi = lax.fori_loop(0, n_i, step, (sr_ref[:, cols], si_ref[:, cols]), unroll=8)
        sr_ref[:, cols] = fr
        si_ref[:, cols] = fi

    if emit:
        for ch, st in halves:
            y_ref[:, ch] = (
                jnp.dot(xr_ref[:, st].astype(BF16), cre_ref[st, ch], preferred_element_type=F32)
                - jnp.dot(xi_ref[:, st].astype(BF16), cim_ref[st, ch], preferred_element_type=F32))
    else:
        @pl.when(ic == pl.num_programs(1) - 1)
        def _():
            er_ref[...] = sr_ref[...]
            ei_ref[...] = si_ref[...]


def s5_scan(u8, bmat_re, bmat_im, a_re, a_im, x0_re, x0_im, cmat_re=None, cmat_im=None,
            rows=256, lane_blk=512):
    s, w = u8.shape
    two = bmat_re.shape[0]
    n_state = bmat_re.shape[-1]
    rows = min(rows, s)
    nblk = s // rows
    emit = cmat_re is not None
    dspec = lambda shape: pl.BlockSpec((None,) + shape, lambda d, i: (d,) + (0,) * len(shape))
    blk = lambda d, i: i + d * (nblk - 1 - 2 * i)
    in_specs = [pl.BlockSpec((rows, w), lambda d, i: (blk(d, i), 0)),
                dspec((w, n_state)), dspec((w, n_state)),
                dspec((1, n_state)), dspec((1, n_state)),
                dspec((N_SEG, n_state)), dspec((N_SEG, n_state))]
    args = [u8, bmat_re, bmat_im, a_re, a_im, x0_re, x0_im]
    if emit:
        in_specs += [dspec((n_state, w)), dspec((n_state, w))]
        args += [cmat_re, cmat_im]
        out_specs = pl.BlockSpec((None, rows, w), lambda d, i: (d, blk(d, i), 0))
        out_shape = jax.ShapeDtypeStruct((two, s, w), F32)
    else:
        out_specs = [dspec((N_SEG, n_state))] * 2
        out_shape = [jax.ShapeDtypeStruct((two, N_SEG, n_state), F32)] * 2
    return pl.pallas_call(
        functools.partial(_s5_scan_kernel, emit=emit, lane_blk=lane_blk),
        grid=(two, s // rows),
        in_specs=in_specs,
        out_specs=out_specs,
        out_shape=out_shape,
        scratch_shapes=[pltpu.VMEM((rows, n_state), F32), pltpu.VMEM((rows, n_state), F32),
                        pltpu.VMEM((N_SEG, n_state), F32), pltpu.VMEM((N_SEG, n_state), F32)],
        compiler_params=_params("parallel", "arbitrary"),
        name="s5_scan_emit" if emit else "s5_scan_ends",
    )(*args)


def _s5_carry_kernel(er_ref, ei_ref, ar_ref, ai_ref, cr_ref, ci_ref, *, seg_len):
    assert seg_len & (seg_len - 1) == 0
    for d in range(er_ref.shape[0]):
        pr, pi = ar_ref[d], ai_ref[d]
        n = seg_len
        while n > 1:
            pr, pi = pr * pr - pi * pi, 2.0 * pr * pi
            n //= 2
        order = range(N_SEG) if d == 0 else range(N_SEG - 1, -1, -1)
        prev = None
        for j in order:
            if prev is None:
                cr = jnp.zeros_like(pr)
                ci = jnp.zeros_like(pi)
            else:
                cr, ci = (er_ref[d, prev:prev + 1, :] + pr * cr - pi * ci,
                          ei_ref[d, prev:prev + 1, :] + pr * ci + pi * cr)
            cr_ref[d, j:j + 1, :] = cr
            ci_ref[d, j:j + 1, :] = ci
            prev = j


def s5_carry(e_re, e_im, a_re, a_im, seg_len):
    shp = jax.ShapeDtypeStruct(e_re.shape, F32)
    return pl.pallas_call(
        functools.partial(_s5_carry_kernel, seg_len=seg_len),
        out_shape=[shp, shp],
        name="s5_carry",
    )(e_re, e_im, a_re, a_im)


def _s5_glu_kernel(u_ref, yf_ref, yb_ref, d_ref, w_ref, b_ref, o_ref):
    width = o_ref.shape[1]
    y = d_ref[...] * u_ref[...].astype(F32) + yf_ref[...] + yb_ref[...]
    c = math.sqrt(2.0 / math.pi)
    y = 0.5 * y * (1.0 + jnp.tanh(c * (y + 0.044715 * (y * y * y))))
    z = jnp.dot(y.astype(BF16), w_ref[...], preferred_element_type=F32) + b_ref[...]
    o_ref[...] = (z[:, :width] * jax.nn.sigmoid(z[:, width:])).astype(o_ref.dtype)


def s5_glu(u, y, d_skip, w_glu, b_glu, tm=1024):
    s, w = u.shape
    tm = min(tm, s)
    return pl.pallas_call(
        _s5_glu_kernel,
        grid=(s // tm,),
        in_specs=[pl.BlockSpec((tm, w), lambda i: (i, 0)),
                  pl.BlockSpec((None, tm, w), lambda i: (0, i, 0)),
                  pl.BlockSpec((None, tm, w), lambda i: (1, i, 0)),
                  pl.BlockSpec((1, w), lambda i: (0, 0)),
                  pl.BlockSpec((w, 2 * w), lambda i: (0, 0)),
                  pl.BlockSpec((1, 2 * w), lambda i: (0, 0))],
        out_specs=pl.BlockSpec((tm, w), lambda i: (i, 0)),
        out_shape=jax.ShapeDtypeStruct((s, w), BF16),
        compiler_params=_params("parallel"),
        name="s5_glu",
    )(u, y, y, d_skip.reshape(1, w), w_glu, b_glu.reshape(1, 2 * w))


def _to_segments(x):
    *lead, s, w = x.shape
    return x.reshape(*lead, N_SEG, s // N_SEG, w).swapaxes(-3, -2).reshape(*lead, s, w)


def _from_segments(x):
    *lead, s, w = x.shape
    return x.reshape(*lead, s // N_SEG, N_SEG, w).swapaxes(-3, -2).reshape(*lead, s, w)


def mixer_b(proj, lam_re, lam_im, log_dt, b_re, b_im, c_re, c_im, d_skip, w_glu, b_glu):
    s = proj.shape[0]
    two, g, p = lam_re.shape
    a_re, a_im, bb_re, bb_im = s5_discretise(lam_re, lam_im, log_dt, b_re, b_im)
    bmat_re = jnp.stack([_block_diag(bb_re[d].swapaxes(1, 2)) for d in range(two)]).astype(BF16)
    bmat_im = jnp.stack([_block_diag(bb_im[d].swapaxes(1, 2)) for d in range(two)]).astype(BF16)
    cmat_re = jnp.stack([_block_diag(c_re[d].swapaxes(1, 2)) for d in range(two)]).astype(BF16)
    cmat_im = jnp.stack([_block_diag(c_im[d].swapaxes(1, 2)) for d in range(two)]).astype(BF16)
    a_re = a_re.reshape(two, 1, g * p)
    a_im = a_im.reshape(two, 1, g * p)
    u8 = _to_segments(proj[:, OFF['bu']:OFF['bu'] + 512])
    zeros = jnp.zeros((two, N_SEG, g * p), F32)
    e_re, e_im = s5_scan(u8, bmat_re, bmat_im, a_re, a_im, zeros, zeros)
    x0_re, x0_im = s5_carry(e_re, e_im, a_re, a_im, s // N_SEG)
    y8 = s5_scan(u8, bmat_re, bmat_im, a_re, a_im, x0_re, x0_im, cmat_re, cmat_im)
    return _from_segments(s5_glu(u8, y8, d_skip, w_glu.astype(BF16), b_glu))


def _flash_scores(q_ref, k_ref, kc, tk, s_ref, slot):
    r = s_ref.shape[1]
    rows = pl.ds(pl.multiple_of(kc * tk, tk), tk)
    q = q_ref[...].reshape(r, LANES)
    s_ref[slot] = lax.dot_general(q, k_ref[rows, :], (((1,), (1,)), ((), ())),
                                  preferred_element_type=F32)


def _flash_update(v_ref, kc, tk, s_ref, slot, p_ref, m_ref, l_ref, acc_ref):
    rows = pl.ds(pl.multiple_of(kc * tk, tk), tk)
    for rb in range(m_ref.shape[0] // FLASH_ROWS):
        rs = slice(rb * FLASH_ROWS, (rb + 1) * FLASH_ROWS)
        tiles = [s_ref[slot, rs, j * LANES:(j + 1) * LANES] for j in range(tk // LANES)]
        m_cur = jnp.max(functools.reduce(jnp.maximum, tiles), axis=-1, keepdims=True)
        m_prev = m_ref[rs]
        m_new = jnp.maximum(m_prev, jnp.broadcast_to(m_cur, m_prev.shape))
        alpha = jnp.exp2(m_prev - m_new)
        ps = [jnp.exp2(t - m_new) for t in tiles]
        l_ref[rs] = alpha * l_ref[rs] + functools.reduce(jnp.add, ps)
        p_ref[rs] = jnp.concatenate(ps, axis=1).astype(BF16)
        acc_ref[rs] = alpha * acc_ref[rs]
        m_ref[rs] = m_new
    acc_ref[...] += jnp.dot(p_ref[...], v_ref[rows, :], preferred_element_type=F32)


def _flash(q_ref, k_ref, v_ref, tk, s_ref, p_ref, m_ref, l_ref, acc_ref):
    m_ref[...] = jnp.full(m_ref.shape, -jnp.inf, F32)
    l_ref[...] = jnp.zeros(l_ref.shape, F32)
    acc_ref[...] = jnp.zeros(acc_ref.shape, F32)
    nk = k_ref.shape[0] // tk
    assert nk % 2 == 0
    _flash_scores(q_ref, k_ref, 0, tk, s_ref, 0)

    def body(j, carry):
        _flash_scores(q_ref, k_ref, 2 * j + 1, tk, s_ref, 1)
        _flash_update(v_ref, 2 * j, tk, s_ref, 0, p_ref, m_ref, l_ref, acc_ref)
        _flash_scores(q_ref, k_ref, jnp.minimum(2 * j + 2, nk - 1), tk, s_ref, 0)
        _flash_update(v_ref, 2 * j + 1, tk, s_ref, 1, p_ref, m_ref, l_ref, acc_ref)
        return carry

    lax.fori_loop(0, nk // 2, body, 0)
    return acc_ref[...] / jnp.sum(l_ref[...], axis=-1, keepdims=True)


def _gqa_kernel(q_ref, k_ref, v_ref, o_ref, s_ref, p_ref, m_ref, l_ref, acc_ref, *, tk):
    nh, tq, _ = q_ref.shape
    o = _flash(q_ref, k_ref, v_ref, tk, s_ref, p_ref, m_ref, l_ref, acc_ref)
    first = lax.broadcasted_iota(jnp.int32, (tq, LANES), 1) < HEAD_DIM
    for pair in range(nh // 2):
        even = o[(2 * pair) * tq:(2 * pair + 1) * tq]
        odd = o[(2 * pair + 1) * tq:(2 * pair + 2) * tq]
        o_ref[:, pair * LANES:(pair + 1) * LANES] = jnp.where(first, even, odd).astype(o_ref.dtype)


def gqa_attn(q_pad, k_dup, v_dup, tq=128, tk=512):
    nh, s, _ = q_pad.shape
    nkv = k_dup.shape[0]
    grp = nh // nkv
    tq, tk = min(tq, s), min(tk, s)
    r = grp * tq
    return pl.pallas_call(
        functools.partial(_gqa_kernel, tk=tk),
        grid=(nkv, s // tq),
        in_specs=[pl.BlockSpec((grp, tq, LANES), lambda g, i: (g, i, 0)),
                  pl.BlockSpec((None, s, LANES), lambda g, i: (g, 0, 0)),
                  pl.BlockSpec((None, s, LANES), lambda g, i: (g, 0, 0))],
        out_specs=pl.BlockSpec((tq, grp * HEAD_DIM), lambda g, i: (i, g)),
        out_shape=jax.ShapeDtypeStruct((s, nh * HEAD_DIM), BF16),
        scratch_shapes=[pltpu.VMEM((2, r, tk), F32), pltpu.VMEM((r, tk), BF16)]
        + [pltpu.VMEM((r, LANES), F32)] * 3,
        compiler_params=_params("parallel", "arbitrary"),
        name="gqa_attn",
    )(q_pad, k_dup, v_dup)


def _diff_kernel(q_ref, k_ref, v_ref, lam_ref, w_ref, o_ref, s_ref, p_ref, m_ref, l_ref, acc_ref, *, tk):
    _, tq, _ = q_ref.shape
    o = _flash(q_ref, k_ref, v_ref, tk, s_ref, p_ref, m_ref, l_ref, acc_ref)
    lp = lam_ref[...]
    lam_init = lp[4:5, 0:1]
    lam = (jnp.exp(jnp.sum(lp[0:1] * lp[1:2], axis=-1, keepdims=True))
           - jnp.exp(jnp.sum(lp[2:3] * lp[3:4], axis=-1, keepdims=True)) + lam_init)
    d = o[:tq] - lam * o[tq:]
    o_ref[...] = (_rms(d, w_ref[...]) * (1.0 - lam_init)).astype(o_ref.dtype)


def diff_attn(q_pad, k, proj, lam_rows, subln_w, tq=256, tk=512):
    n2, s, _ = q_pad.shape
    nh = n2 // 2
    tq, tk = min(tq, s), min(tk, s)
    vb = OFF['dv'] // LANES
    return pl.pallas_call(
        functools.partial(_diff_kernel, tk=tk),
        grid=(nh, s // tq),
        in_specs=[pl.BlockSpec((2, tq, LANES), lambda h, i: (h, i, 0)),
                  pl.BlockSpec((s, LANES), lambda h, i: (0, h)),
                  pl.BlockSpec((s, LANES), lambda h, i: (0, vb + h)),
                  pl.BlockSpec((8, LANES), lambda h, i: (0, 0)),
                  pl.BlockSpec((1, LANES), lambda h, i: (0, 0))],
        out_specs=pl.BlockSpec((tq, LANES), lambda h, i: (i, h)),
        out_shape=jax.ShapeDtypeStruct((s, nh * LANES), BF16),
        scratch_shapes=[pltpu.VMEM((2, 2 * tq, tk), F32), pltpu.VMEM((2 * tq, tk), BF16)]
        + [pltpu.VMEM((2 * tq, LANES), F32)] * 3,
        compiler_params=_params("parallel", "arbitrary"),
        name="diff_attn",
    )(q_pad, k, proj, lam_rows, subln_w.reshape(1, LANES))


def _out_ple_kernel(x_ref, a_ref, b_ref, c_ref, d_ref, g_ref, p_ref, wo_ref, nw_ref, gw_ref, pw_ref,
                    o_ref):
    g = g_ref[...].astype(F32)
    mix = jnp.concatenate([a_ref[...], b_ref[...], c_ref[...], d_ref[...]], axis=-1).astype(F32)
    mixed = (mix * (g * jax.nn.sigmoid(g))).astype(BF16)
    x1 = x_ref[...] + jnp.dot(mixed, wo_ref[...], preferred_element_type=F32)
    h = _rms(x1, nw_ref[...]).astype(BF16)
    gate = jax.nn.sigmoid(jnp.dot(h, gw_ref[...], preferred_element_type=F32))
    pe = jnp.dot(p_ref[...].astype(BF16), pw_ref[...], preferred_element_type=F32)
    o_ref[...] = x1 + gate * pe


def out_ple(x, a, b, c, d, proj, p, w_out, ple_norm_w, ple_gate_w, ple_w, tm=256):
    s, dm = x.shape
    bw = a.shape[1]
    pd = p.shape[1]
    tm = min(tm, s)
    gb = OFF['ag'] // dm
    row = lambda w: pl.BlockSpec((tm, w), lambda i: (i, 0))
    const = lambda r, c_: pl.BlockSpec((r, c_), lambda i: (0, 0), pipeline_mode=pl.Buffered(1))
    return pl.pallas_call(
        _out_ple_kernel,
        grid=(s // tm,),
        in_specs=[row(dm), row(bw), row(bw), row(bw), row(bw),
                  pl.BlockSpec((tm, dm), lambda i: (i, gb)), row(pd),
                  const(dm, dm), const(1, dm), const(dm, dm), const(pd, dm)],
        out_specs=row(dm),
        out_shape=jax.ShapeDtypeStruct((s, dm), F32),
        compiler_params=_params("parallel"),
        name="out_ple",
    )(x, a, b, c, d, proj, p, w_out, ple_norm_w.reshape(1, dm), ple_gate_w, ple_w)


def _reorder_cols(w):
    return jnp.concatenate([w[:, _ORIG[n][0]:_ORIG[n][0] + _ORIG[n][1]] for n in _ORDER], axis=1)


def _pair_weights(ws, scale):
    return jnp.stack([jnp.tile(w.astype(F32) * scale, 2).reshape(1, LANES) for w in ws])


def kernel(x, p, norm_w, w_in, w_out, a_q_norm, a_k_norm, s5_lambda_re, s5_lambda_im, s5_log_dt, s5_b_re, s5_b_im, s5_c_re, s5_c_im, s5_d, s5_w_glu, s5_b_glu, c_q_norm, c_k_norm, d_q_norm, d_k_norm, d_lambda_q1, d_lambda_k1, d_lambda_q2, d_lambda_k2, d_subln, ple_norm_w, ple_gate_w, ple_w):
    bsz, s, dm = x.shape
    assert bsz == 1
    depth = w_in.shape[0]
    scale = HEAD_DIM ** -0.5
    scale2 = scale * math.log2(math.e)
    tab_1d = _rope_table_1d(s)
    tab_ax = _rope_table_axial(s)
    blk = lambda name, n: tuple(OFF[name] // LANES + j for j in range(n))
    xs = x[0]
    for i in range(depth):
        proj = in_proj(xs, norm_w[i], _reorder_cols(w_in[i].astype(BF16)))

        q0 = qk_prep(proj, blk('aq', 4) + blk('dq', 4),
                     jnp.concatenate([_pair_weights([a_q_norm[i]] * 4, scale),
                                      _pair_weights([d_q_norm[i]] * 4, scale2)]),
                     tab_1d, ROT_DIMS // 2, 'q')
        k0 = qk_prep(proj, blk('ak', 4) + blk('dk', 4),
                     _pair_weights([a_k_norm[i]] * 4 + [d_k_norm[i]] * 4, 1.0), tab_1d, ROT_DIMS // 2, 'k')
        q1 = qk_prep(proj, blk('cq', 4), _pair_weights([c_q_norm[i]] * 4, scale2), tab_ax,
                     HEAD_DIM // 4, 'q')
        k1 = qk_prep(proj, blk('ck', 1), _pair_weights([c_k_norm[i]], 1.0), tab_ax, HEAD_DIM // 4, 'kdup')

        a_out = mixer_a(q0[:8], k0[:, :512], proj)

        b_out = mixer_b(proj, s5_lambda_re[i], s5_lambda_im[i], s5_log_dt[i], s5_b_re[i], s5_b_im[i],
                        s5_c_re[i], s5_c_im[i], s5_d[i], s5_w_glu[i], s5_b_glu[i])

        cv = proj[:, OFF['cv']:OFF['cv'] + LANES]
        v_dup = jnp.stack([jnp.tile(cv[:, :HEAD_DIM], (1, 2)), jnp.tile(cv[:, HEAD_DIM:], (1, 2))])
        c_out = gqa_attn(q1, k1, v_dup)

        lam_init = 0.8 - 0.6 * math.exp(-0.3 * i)
        pad = lambda v: jnp.pad(v.astype(F32), (0, LANES - HEAD_DIM))
        lam_rows = jnp.stack([pad(d_lambda_q1[i]), pad(d_lambda_k1[i]), pad(d_lambda_q2[i]),
                              pad(d_lambda_k2[i]), jnp.full((LANES,), lam_init, F32)]
                             + [jnp.zeros((LANES,), F32)] * 3)
        d_out = diff_attn(q0[8:], k0[:, 512:], proj, lam_rows, d_subln[i])

        xs = out_ple(xs, a_out, b_out, c_out, d_out, proj, p[i, 0], w_out[i].astype(BF16),
                     ple_norm_w[i], ple_gate_w[i].astype(BF16), ple_w[i].astype(BF16))
    return xs[None]
```

```python
import functools
import math

import jax
import jax.numpy as jnp
import numpy as np
from jax import lax
from jax.experimental import pallas as pl
from jax.experimental.pallas import tpu as pltpu

F32 = jnp.float32
BF16 = jnp.bfloat16

HEAD_DIM = 64
LANES = 128
NORM_EPS = 1e-6
MASK_VALUE = -1e30
ROPE_THETA = 500000.0
AXIAL_THETA = 10000.0
ROT_DIMS = HEAD_DIM // 4
GRID_W = 64
DILATIONS = (1, 4, 16)
N_SIDE = 64
SSM_GROUP = 16
SSM_STATE = 64
N_SEG = 8
FLASH_ROWS = 64
FLASH_CHUNKS_PER_TRIP = 8
VMEM_LIMIT = 56 * 1024 * 1024

_ORIG = dict(aq=(0, 512), ak=(512, 512), av=(1024, 512), ag=(1536, 512), bu=(2048, 512),
             bg=(2560, 512), cq=(3072, 512), ck=(3584, 128), cv=(3712, 128), cg=(3840, 512),
             dq=(4352, 512), dk=(4864, 512), dv=(5376, 512), dg=(5888, 512))
_ORDER = ('aq', 'ak', 'dq', 'dk', 'cq', 'av', 'dv', 'bu', 'ag', 'bg', 'cg', 'dg', 'ck', 'cv')
OFF = {}
_o = 0
for _n in _ORDER:
    OFF[_n] = _o
    _o += _ORIG[_n][1]
IN_COLS = _o


def _params(*sem):
    return pltpu.CompilerParams(dimension_semantics=sem, vmem_limit_bytes=VMEM_LIMIT)


def _rms(x, w):
    return x * lax.rsqrt(jnp.mean(x * x, axis=-1, keepdims=True) + NORM_EPS) * w


def _in_proj_kernel(x_ref, nw_ref, w_ref, o_ref, h_ref):
    @pl.when(pl.program_id(1) == 0)
    def _():
        h_ref[...] = _rms(x_ref[...], nw_ref[...]).astype(BF16)

    o_ref[...] = jnp.dot(h_ref[...], w_ref[...], preferred_element_type=F32).astype(o_ref.dtype)


def in_proj(x, norm_w, w, tm=1024, tn=640):
    s, d = x.shape
    n = w.shape[1]
    tm = min(tm, s)
    return pl.pallas_call(
        _in_proj_kernel,
        grid=(s // tm, n // tn),
        in_specs=[pl.BlockSpec((tm, d), lambda i, j: (i, 0)),
                  pl.BlockSpec((1, d), lambda i, j: (0, 0)),
                  pl.BlockSpec((d, tn), lambda i, j: (0, j))],
        out_specs=pl.BlockSpec((tm, tn), lambda i, j: (i, j)),
        out_shape=jax.ShapeDtypeStruct((s, n), BF16),
        scratch_shapes=[pltpu.VMEM((tm, d), BF16)],
        compiler_params=_params("parallel", "arbitrary"),
        name="in_proj",
    )(x, norm_w.reshape(1, d), w)


def _qk_prep_kernel(cb_ref, x_ref, w_ref, g_ref, t_ref, o_ref, *, shift, mode):
    del cb_ref
    x = x_ref[...].astype(F32)
    sq = x * x
    hi = sq.astype(BF16)
    lo = (sq - hi.astype(F32)).astype(BF16)
    g = g_ref[...]
    ms = jnp.dot(hi, g, preferred_element_type=F32) + jnp.dot(lo, g, preferred_element_type=F32)
    y = x * lax.rsqrt(ms + NORM_EPS) * w_ref[...]
    y = (y * t_ref[0] + pltpu.roll(y, shift, 1) * t_ref[1]
         + pltpu.roll(y, LANES - shift, 1) * t_ref[2])
    first = lax.broadcasted_iota(jnp.int32, y.shape, 1) < HEAD_DIM
    if mode == 'q':
        o_ref[0] = jnp.where(first, y, 0.0).astype(o_ref.dtype)
        o_ref[1] = jnp.where(first, 0.0, y).astype(o_ref.dtype)
    elif mode == 'k':
        o_ref[...] = y.astype(o_ref.dtype)
    else:
        r = pltpu.roll(y, HEAD_DIM, 1)
        o_ref[0] = jnp.where(first, y, r).astype(o_ref.dtype)
        o_ref[1] = jnp.where(first, r, y).astype(o_ref.dtype)


def qk_prep(proj, col_blocks, weights, table, shift, mode, out_dtype=BF16, tm=1024):
    s = proj.shape[0]
    tm = min(tm, s)
    nb = len(col_blocks)
    cb = jnp.asarray(col_blocks, jnp.int32)
    gmat = jnp.asarray(np.kron(np.eye(2), np.full((HEAD_DIM, HEAD_DIM), 1.0 / HEAD_DIM)), BF16)
    if mode == 'k':
        out_shape = jax.ShapeDtypeStruct((s, nb * LANES), out_dtype)
        out_spec = pl.BlockSpec((tm, LANES), lambda i, j, cb: (i, j))
    else:
        out_shape = jax.ShapeDtypeStruct((2 * nb, s, LANES), out_dtype)
        out_spec = pl.BlockSpec((2, tm, LANES), lambda i, j, cb: (j, i, 0))
    grid_spec = pltpu.PrefetchScalarGridSpec(
        num_scalar_prefetch=1,
        grid=(s // tm, nb),
        in_specs=[pl.BlockSpec((tm, LANES), lambda i, j, cb: (i, cb[j])),
                  pl.BlockSpec((None, 1, LANES), lambda i, j, cb: (j, 0, 0)),
                  pl.BlockSpec((LANES, LANES), lambda i, j, cb: (0, 0)),
                  pl.BlockSpec((3, tm, LANES), lambda i, j, cb: (0, i, 0))],
        out_specs=out_spec,
    )
    return pl.pallas_call(
        functools.partial(_qk_prep_kernel, shift=shift, mode=mode),
        grid_spec=grid_spec,
        out_shape=out_shape,
        compiler_params=_params("parallel", "arbitrary"),
        name="qk_prep_" + mode,
    )(cb, proj, weights, gmat, table)


def _rope_tables(pos, n_dims, theta):
    inv = theta ** (-jnp.arange(0, n_dims, 2, dtype=F32) / n_dims)
    ang = pos[:, None] * inv[None, :]
    return jnp.cos(ang), jnp.sin(ang)


def _rope_table_1d(s):
    cos, sin = _rope_tables(jnp.arange(s).astype(F32), ROT_DIMS, ROPE_THETA)
    z8 = jnp.zeros_like(cos)
    rest = HEAD_DIM - ROT_DIMS
    c = jnp.concatenate([cos, cos, jnp.ones((s, rest), F32)], axis=1)
    s1 = jnp.concatenate([z8, sin, jnp.zeros((s, rest), F32)], axis=1)
    s2 = jnp.concatenate([-sin, z8, jnp.zeros((s, rest), F32)], axis=1)
    return jnp.stack([jnp.tile(t, (1, 2)) for t in (c, s1, s2)])


def _rope_table_axial(s):
    t = jnp.arange(s)
    rows = s // GRID_W
    row_c = (t // GRID_W - rows // 2).astype(F32)
    col_c = (t % GRID_W - GRID_W // 2).astype(F32)
    cr, sr = _rope_tables(row_c, HEAD_DIM // 2, AXIAL_THETA)
    cc, sc = _rope_tables(col_c, HEAD_DIM // 2, AXIAL_THETA)
    z = jnp.zeros_like(cr)
    c = jnp.concatenate([cr, cr, cc, cc], axis=1)
    s1 = jnp.concatenate([z, sr, z, sc], axis=1)
    s2 = jnp.concatenate([-sr, z, -sc, z], axis=1)
    return jnp.stack([jnp.tile(t_, (1, 2)) for t_ in (c, s1, s2)])


def _band_attn_kernel(q_ref, k_ref, v_ref, o_ref, os_ref, ls_ref, *, tq, comb_rows):
    n_rows = k_ref.shape[0]
    span = q_ref.shape[0]
    win = tq + 2 * N_SIDE
    sp = pl.program_id(1)
    first = lax.broadcasted_iota(jnp.int32, (tq, LANES), 1) < HEAD_DIM
    row = lax.broadcasted_iota(jnp.int32, (2 * tq, win), 0)
    row = jnp.where(row >= tq, row - tq, row)
    rel = lax.broadcasted_iota(jnp.int32, (2 * tq, win), 1) - row

    for pi, d in enumerate(DILATIONS):
        seg_len = n_rows // d
        per_class = span // (tq * d)

        def tile(tid, carry, pi=pi, d=d, seg_len=seg_len, per_class=per_class):
            r = tid // per_class
            ti = tid % per_class
            i0 = sp * (span // d) + ti * tq
            c0 = jnp.clip(i0 - N_SIDE, 0, seg_len - win)
            q_rows = pl.ds(r + d * ti * tq, tq, stride=d)
            k_rows = pl.ds(r + d * c0, win, stride=d)
            q = q_ref[q_rows, :]
            q2 = jnp.concatenate([jnp.where(first, q, 0.0), jnp.where(first, 0.0, q)], axis=0)
            kw = k_ref[k_rows, :].astype(BF16)
            vw = v_ref[k_rows, :].astype(BF16)
            s = lax.dot_general(q2.astype(BF16), kw, (((1,), (1,)), ((), ())),
                                preferred_element_type=F32)
            s = jnp.where(jnp.abs(rel + (c0 - i0)) <= N_SIDE, s, MASK_VALUE)
            m = jnp.max(s, axis=-1, keepdims=True)
            p = jnp.exp(s - m)
            l = jnp.sum(p, axis=-1, keepdims=True)
            o = jnp.dot(p.astype(BF16), vw, preferred_element_type=F32) / l
            lse = m + jnp.log(l)
            os_ref[pi, q_rows, :] = jnp.where(first, o[:tq], o[tq:])
            ls_ref[pi, q_rows, :] = jnp.where(first, lse[:tq], lse[tq:])
            return carry

        lax.fori_loop(0, span // tq, tile, 0, unroll=4)

    def combine(c, carry):
        rows = pl.ds(pl.multiple_of(c * comb_rows, comb_rows), comb_rows)
        la, lb, lc = ls_ref[0, rows, :], ls_ref[1, rows, :], ls_ref[2, rows, :]
        m = jnp.maximum(jnp.maximum(la, lb), lc)
        ea, eb, ec = jnp.exp(la - m), jnp.exp(lb - m), jnp.exp(lc - m)
        num = ea * os_ref[0, rows, :] + eb * os_ref[1, rows, :] + ec * os_ref[2, rows, :]
        o_ref[rows, :] = (num / (ea + eb + ec)).astype(o_ref.dtype)
        return carry

    lax.fori_loop(0, span // comb_rows, combine, 0)


def band_attn(qk, v, tq=128, comb_rows=256):
    s = qk.shape[0]
    n_pairs = v.shape[1] // LANES
    span = tq * max(DILATIONS)
    assert s % span == 0 and s // max(DILATIONS) >= tq + 2 * N_SIDE
    return pl.pallas_call(
        functools.partial(_band_attn_kernel, tq=tq, comb_rows=comb_rows),
        grid=(n_pairs, s // span),
        in_specs=[pl.BlockSpec((span, LANES), lambda m, c: (c, m)),
                  pl.BlockSpec((s, LANES), lambda m, c: (0, n_pairs + m)),
                  pl.BlockSpec((s, LANES), lambda m, c: (0, m))],
        out_specs=pl.BlockSpec((span, LANES), lambda m, c: (c, m)),
        out_shape=jax.ShapeDtypeStruct((s, n_pairs * LANES), BF16),
        scratch_shapes=[pltpu.VMEM((len(DILATIONS), span, LANES), F32)] * 2,
        compiler_params=_params("parallel", "arbitrary"),
        name="band_attn",
    )(qk, qk, v)


def _s5_disc_kernel(lr_ref, li_ref, ldt_ref, br_ref, bi_ref, ar_ref, ai_ref, bbr_ref, bbi_ref):
    lr, li = lr_ref[...], li_ref[...]
    dt = jnp.exp(ldt_ref[...])
    mag = jnp.exp(lr * dt)
    ar = mag * jnp.cos(li * dt)
    ai = mag * jnp.sin(li * dt)
    den = lr * lr + li * li
    cre = ((ar - 1.0) * lr + ai * li) / den
    cim = (ai * lr - (ar - 1.0) * li) / den
    br, bi = br_ref[...], bi_ref[...]
    ar_ref[...] = ar
    ai_ref[...] = ai
    bbr_ref[...] = cre * br - cim * bi
    bbi_ref[...] = cre * bi + cim * br


def s5_discretise(lam_re, lam_im, log_dt, b_re, b_im):
    two, g, p = lam_re.shape
    c = b_re.shape[-1]
    rep = lambda t: jnp.repeat(t.reshape(two * g, p), c, axis=1)
    shp = jax.ShapeDtypeStruct((two * g, p * c), F32)
    ar, ai, bbr, bbi = pl.pallas_call(
        _s5_disc_kernel,
        out_shape=[shp] * 4,
        name="s5_disc",
    )(rep(lam_re), rep(lam_im), log_dt.reshape(two * g, 1),
      b_re.reshape(two * g, p * c), b_im.reshape(two * g, p * c))
    unrep = lambda t: t.reshape(two, g, p, c)[..., 0]
    return unrep(ar), unrep(ai), bbr.reshape(two, g, p, c), bbi.reshape(two, g, p, c)


def _block_diag(m):
    g, r, c = m.shape
    eye = jnp.eye(g, dtype=m.dtype)
    return jnp.einsum('grc,gh->grhc', m, eye).reshape(g * r, g * c)


def _s5_scan_kernel(u_ref, bre_ref, bim_ref, ar_ref, ai_ref, x0r_ref, x0i_ref, *rest,
                    emit, lane_blk):
    if emit:
        cre_ref, cim_ref, y_ref, xr_ref, xi_ref, sr_ref, si_ref = rest
    else:
        er_ref, ei_ref, xr_ref, xi_ref, sr_ref, si_ref = rest
    back = pl.program_id(0)
    ic = pl.program_id(1)
    n_i = u_ref.shape[0] // N_SEG
    n_state = xr_ref.shape[1]

    @pl.when(ic == 0)
    def _():
        sr_ref[...] = x0r_ref[...]
        si_ref[...] = x0i_ref[...]

    w = u_ref.shape[1]
    halves = [(slice(h * w // 2, (h + 1) * w // 2), slice(h * n_state // 2, (h + 1) * n_state // 2))
              for h in range(2)]
    for ch, st in halves:
        u = u_ref[:, ch]
        xr_ref[:, st] = jnp.dot(u, bre_ref[ch, st], preferred_element_type=F32)
        xi_ref[:, st] = jnp.dot(u, bim_ref[ch, st], preferred_element_type=F32)

    for lb in range(n_state // lane_blk):
        cols = pl.ds(lb * lane_blk, lane_blk)
        ar = jnp.broadcast_to(ar_ref[:, cols], (N_SEG, lane_blk))
        ai = jnp.broadcast_to(ai_ref[:, cols], (N_SEG, lane_blk))

        def step(i, carry):
            xr, xi = carry
            i = i + back * (n_i - 1 - 2 * i)
            rows = pl.ds(pl.multiple_of(i * N_SEG, N_SEG), N_SEG)
            nr = ar * xr - ai * xi + xr_ref[rows, cols]
            ni = ar * xi + ai * xr + xi_ref[rows, cols]
            xr_ref[rows, cols] = nr
            xi_ref[rows, cols] = ni
            return nr, ni

        fr, fi = lax.fori_loop(0, n_i, step, (sr_ref[:, cols], si_ref[:, cols]), unroll=8)
        sr_ref[:, cols] = fr
        si_ref[:, cols] = fi

    if emit:
        for ch, st in halves:
            y_ref[:, ch] = (
                jnp.dot(xr_ref[:, st].astype(BF16), cre_ref[st, ch], preferred_element_type=F32)
                - jnp.dot(xi_ref[:, st].astype(BF16), cim_ref[st, ch], preferred_element_type=F32))
    else:
        @pl.when(ic == pl.num_programs(1) - 1)
        def _():
            er_ref[...] = sr_ref[...]
            ei_ref[...] = si_ref[...]


def s5_scan(u8, bmat_re, bmat_im, a_re, a_im, x0_re, x0_im, cmat_re=None, cmat_im=None,
            rows=256, lane_blk=512):
    s, w = u8.shape
    two = bmat_re.shape[0]
    n_state = bmat_re.shape[-1]
    rows = min(rows, s)
    nblk = s // rows
    emit = cmat_re is not None
    dspec = lambda shape: pl.BlockSpec((None,) + shape, lambda d, i: (d,) + (0,) * len(shape))
    blk = lambda d, i: i + d * (nblk - 1 - 2 * i)
    in_specs = [pl.BlockSpec((rows, w), lambda d, i: (blk(d, i), 0)),
                dspec((w, n_state)), dspec((w, n_state)),
                dspec((1, n_state)), dspec((1, n_state)),
                dspec((N_SEG, n_state)), dspec((N_SEG, n_state))]
    args = [u8, bmat_re, bmat_im, a_re, a_im, x0_re, x0_im]
    if emit:
        in_specs += [dspec((n_state, w)), dspec((n_state, w))]
        args += [cmat_re, cmat_im]
        out_specs = pl.BlockSpec((None, rows, w), lambda d, i: (d, blk(d, i), 0))
        out_shape = jax.ShapeDtypeStruct((two, s, w), F32)
    else:
        out_specs = [dspec((N_SEG, n_state))] * 2
        out_shape = [jax.ShapeDtypeStruct((two, N_SEG, n_state), F32)] * 2
    return pl.pallas_call(
        functools.partial(_s5_scan_kernel, emit=emit, lane_blk=lane_blk),
        grid=(two, s // rows),
        in_specs=in_specs,
        out_specs=out_specs,
        out_shape=out_shape,
        scratch_shapes=[pltpu.VMEM((rows, n_state), F32), pltpu.VMEM((rows, n_state), F32),
                        pltpu.VMEM((N_SEG, n_state), F32), pltpu.VMEM((N_SEG, n_state), F32)],
        compiler_params=_params("parallel", "arbitrary"),
        name="s5_scan_emit" if emit else "s5_scan_ends",
    )(*args)


def _s5_carry_kernel(er_ref, ei_ref, ar_ref, ai_ref, cr_ref, ci_ref, *, seg_len):
    assert seg_len & (seg_len - 1) == 0
    for d in range(er_ref.shape[0]):
        pr, pi = ar_ref[d], ai_ref[d]
        n = seg_len
        while n > 1:
            pr, pi = pr * pr - pi * pi, 2.0 * pr * pi
            n //= 2
        order = range(N_SEG) if d == 0 else range(N_SEG - 1, -1, -1)
        prev = None
        for j in order:
            if prev is None:
                cr = jnp.zeros_like(pr)
                ci = jnp.zeros_like(pi)
            else:
                cr, ci = (er_ref[d, prev:prev + 1, :] + pr * cr - pi * ci,
                          ei_ref[d, prev:prev + 1, :] + pr * ci + pi * cr)
            cr_ref[d, j:j + 1, :] = cr
            ci_ref[d, j:j + 1, :] = ci
            prev = j


def s5_carry(e_re, e_im, a_re, a_im, seg_len):
    shp = jax.ShapeDtypeStruct(e_re.shape, F32)
    return pl.pallas_call(
        functools.partial(_s5_carry_kernel, seg_len=seg_len),
        out_shape=[shp, shp],
        name="s5_carry",
    )(e_re, e_im, a_re, a_im)


def _s5_glu_kernel(u_ref, yf_ref, yb_ref, d_ref, w_ref, b_ref, o_ref):
    width = o_ref.shape[1]
    y = d_ref[...] * u_ref[...].astype(F32) + yf_ref[...] + yb_ref[...]
    c = math.sqrt(2.0 / math.pi)
    y = 0.5 * y * (1.0 + jnp.tanh(c * (y + 0.044715 * (y * y * y))))
    z = jnp.dot(y.astype(BF16), w_ref[...], preferred_element_type=F32) + b_ref[...]
    o_ref[...] = (z[:, :width] * jax.nn.sigmoid(z[:, width:])).astype(o_ref.dtype)


def s5_glu(u, y, d_skip, w_glu, b_glu, tm=1024):
    s, w = u.shape
    tm = min(tm, s)
    return pl.pallas_call(
        _s5_glu_kernel,
        grid=(s // tm,),
        in_specs=[pl.BlockSpec((tm, w), lambda i: (i, 0)),
                  pl.BlockSpec((None, tm, w), lambda i: (0, i, 0)),
                  pl.BlockSpec((None, tm, w), lambda i: (1, i, 0)),
                  pl.BlockSpec((1, w), lambda i: (0, 0)),
                  pl.BlockSpec((w, 2 * w), lambda i: (0, 0)),
                  pl.BlockSpec((1, 2 * w), lambda i: (0, 0))],
        out_specs=pl.BlockSpec((tm, w), lambda i: (i, 0)),
        out_shape=jax.ShapeDtypeStruct((s, w), BF16),
        compiler_params=_params("parallel"),
        name="s5_glu",
    )(u, y, y, d_skip.reshape(1, w), w_glu, b_glu.reshape(1, 2 * w))


def _to_segments(x):
    *lead, s, w = x.shape
    return x.reshape(*lead, N_SEG, s // N_SEG, w).swapaxes(-3, -2).reshape(*lead, s, w)


def _from_segments(x):
    *lead, s, w = x.shape
    return x.reshape(*lead, s // N_SEG, N_SEG, w).swapaxes(-3, -2).reshape(*lead, s, w)


def mixer_b(proj, lam_re, lam_im, log_dt, b_re, b_im, c_re, c_im, d_skip, w_glu, b_glu):
    s = proj.shape[0]
    two, g, p = lam_re.shape
    a_re, a_im, bb_re, bb_im = s5_discretise(lam_re, lam_im, log_dt, b_re, b_im)
    bmat_re = jnp.stack([_block_diag(bb_re[d].swapaxes(1, 2)) for d in range(two)]).astype(BF16)
    bmat_im = jnp.stack([_block_diag(bb_im[d].swapaxes(1, 2)) for d in range(two)]).astype(BF16)
    cmat_re = jnp.stack([_block_diag(c_re[d].swapaxes(1, 2)) for d in range(two)]).astype(BF16)
    cmat_im = jnp.stack([_block_diag(c_im[d].swapaxes(1, 2)) for d in range(two)]).astype(BF16)
    a_re = a_re.reshape(two, 1, g * p)
    a_im = a_im.reshape(two, 1, g * p)
    u8 = _to_segments(proj[:, OFF['bu']:OFF['bu'] + 512])
    zeros = jnp.zeros((two, N_SEG, g * p), F32)
    e_re, e_im = s5_scan(u8, bmat_re, bmat_im, a_re, a_im, zeros, zeros)
    x0_re, x0_im = s5_carry(e_re, e_im, a_re, a_im, s // N_SEG)
    y8 = s5_scan(u8, bmat_re, bmat_im, a_re, a_im, x0_re, x0_im, cmat_re, cmat_im)
    return _from_segments(s5_glu(u8, y8, d_skip, w_glu.astype(BF16), b_glu))


def _flash_scores(q_ref, k_ref, kc, tk, s_ref, slot):
    r = s_ref.shape[1]
    rows = pl.ds(pl.multiple_of(kc * tk, tk), tk)
    q = q_ref[...].reshape(r, LANES)
    s_ref[slot] = lax.dot_general(q, k_ref[rows, :], (((1,), (1,)), ((), ())),
                                  preferred_element_type=F32)


def _flash_update(v_ref, kc, tk, s_ref, slot, p_ref, m_ref, l_ref, acc_ref):
    rows = pl.ds(pl.multiple_of(kc * tk, tk), tk)
    for rb in range(m_ref.shape[0] // FLASH_ROWS):
        rs = slice(rb * FLASH_ROWS, (rb + 1) * FLASH_ROWS)
        tiles = [s_ref[slot, rs, j * LANES:(j + 1) * LANES] for j in range(tk // LANES)]
        m_cur = jnp.max(functools.reduce(jnp.maximum, tiles), axis=-1, keepdims=True)
        m_prev = m_ref[rs]
        m_new = jnp.maximum(m_prev, jnp.broadcast_to(m_cur, m_prev.shape))
        alpha = jnp.exp2(m_prev - m_new)
        ps = [jnp.exp2(t - m_new) for t in tiles]
        l_ref[rs] = alpha * l_ref[rs] + functools.reduce(jnp.add, ps)
        p_ref[rs] = jnp.concatenate(ps, axis=1).astype(BF16)
        acc_ref[rs] = alpha * acc_ref[rs]
        m_ref[rs] = m_new
    acc_ref[...] += jnp.dot(p_ref[...], v_ref[rows, :], preferred_element_type=F32)


def _flash(q_ref, k_ref, v_ref, tk, s_ref, p_ref, m_ref, l_ref, acc_ref):
    m_ref[...] = jnp.full(m_ref.shape, -jnp.inf, F32)
    l_ref[...] = jnp.zeros(l_ref.shape, F32)
    acc_ref[...] = jnp.zeros(acc_ref.shape, F32)
    nk = k_ref.shape[0] // tk
    per_trip = min(FLASH_CHUNKS_PER_TRIP, nk)
    assert nk % per_trip == 0 and per_trip % 2 == 0
    _flash_scores(q_ref, k_ref, 0, tk, s_ref, 0)

    def body(j, carry):
        for c in range(per_trip):
            kc = per_trip * j + c
            nxt = kc + 1 if c + 1 < per_trip else jnp.minimum(kc + 1, nk - 1)
            _flash_scores(q_ref, k_ref, nxt, tk, s_ref, (c + 1) % 2)
            _flash_update(v_ref, kc, tk, s_ref, c % 2, p_ref, m_ref, l_ref, acc_ref)
        return carry

    lax.fori_loop(0, nk // per_trip, body, 0)
    return acc_ref[...] / jnp.sum(l_ref[...], axis=-1, keepdims=True)


def _gqa_kernel(q_ref, k_ref, v_ref, o_ref, s_ref, p_ref, m_ref, l_ref, acc_ref, *, tk):
    nh, tq, _ = q_ref.shape
    o = _flash(q_ref, k_ref, v_ref, tk, s_ref, p_ref, m_ref, l_ref, acc_ref)
    first = lax.broadcasted_iota(jnp.int32, (tq, LANES), 1) < HEAD_DIM
    for pair in range(nh // 2):
        even = o[(2 * pair) * tq:(2 * pair + 1) * tq]
        odd = o[(2 * pair + 1) * tq:(2 * pair + 2) * tq]
        o_ref[:, pair * LANES:(pair + 1) * LANES] = jnp.where(first, even, odd).astype(o_ref.dtype)


def gqa_attn(q_pad, k_dup, v_dup, tq=128, tk=1024):
    nh, s, _ = q_pad.shape
    nkv = k_dup.shape[0]
    grp = nh // nkv
    tq, tk = min(tq, s), min(tk, s)
    r = grp * tq
    return pl.pallas_call(
        functools.partial(_gqa_kernel, tk=tk),
        grid=(nkv, s // tq),
        in_specs=[pl.BlockSpec((grp, tq, LANES), lambda g, i: (g, i, 0)),
                  pl.BlockSpec((None, s, LANES), lambda g, i: (g, 0, 0)),
                  pl.BlockSpec((None, s, LANES), lambda g, i: (g, 0, 0))],
        out_specs=pl.BlockSpec((tq, grp * HEAD_DIM), lambda g, i: (i, g)),
        out_shape=jax.ShapeDtypeStruct((s, nh * HEAD_DIM), BF16),
        scratch_shapes=[pltpu.VMEM((2, r, tk), F32), pltpu.VMEM((r, tk), BF16)]
        + [pltpu.VMEM((r, LANES), F32)] * 3,
        compiler_params=_params("parallel", "arbitrary"),
        name="gqa_attn",
    )(q_pad, k_dup, v_dup)


def _diff_kernel(q_ref, k_ref, v_ref, lam_ref, w_ref, o_ref, s_ref, p_ref, m_ref, l_ref, acc_ref, *, tk):
    _, tq, _ = q_ref.shape
    o = _flash(q_ref, k_ref, v_ref, tk, s_ref, p_ref, m_ref, l_ref, acc_ref)
    lp = lam_ref[...]
    lam_init = lp[4:5, 0:1]
    lam = (jnp.exp(jnp.sum(lp[0:1] * lp[1:2], axis=-1, keepdims=True))
           - jnp.exp(jnp.sum(lp[2:3] * lp[3:4], axis=-1, keepdims=True)) + lam_init)
    d = o[:tq] - lam * o[tq:]
    o_ref[...] = (_rms(d, w_ref[...]) * (1.0 - lam_init)).astype(o_ref.dtype)


def diff_attn(q_pad, k, proj, lam_rows, subln_w, tq=256, tk=1024):
    n2, s, _ = q_pad.shape
    nh = n2 // 2
    tq, tk = min(tq, s), min(tk, s)
    vb = OFF['dv'] // LANES
    return pl.pallas_call(
        functools.partial(_diff_kernel, tk=tk),
        grid=(nh, s // tq),
        in_specs=[pl.BlockSpec((2, tq, LANES), lambda h, i: (h, i, 0)),
                  pl.BlockSpec((s, LANES), lambda h, i: (0, h)),
                  pl.BlockSpec((s, LANES), lambda h, i: (0, vb + h)),
                  pl.BlockSpec((8, LANES), lambda h, i: (0, 0)),
                  pl.BlockSpec((1, LANES), lambda h, i: (0, 0))],
        out_specs=pl.BlockSpec((tq, LANES), lambda h, i: (i, h)),
        out_shape=jax.ShapeDtypeStruct((s, nh * LANES), BF16),
        scratch_shapes=[pltpu.VMEM((2, 2 * tq, tk), F32), pltpu.VMEM((2 * tq, tk), BF16)]
        + [pltpu.VMEM((2 * tq, LANES), F32)] * 3,
        compiler_params=_params("parallel", "arbitrary"),
        name="diff_attn",
    )(q_pad, k, proj, lam_rows, subln_w.reshape(1, LANES))


def _out_ple_kernel(x_ref, a_ref, b_ref, c_ref, d_ref, g_ref, p_ref, wo_ref, nw_ref, gw_ref, pw_ref,
                    o_ref):
    g = g_ref[...].astype(F32)
    mix = jnp.concatenate([a_ref[...], b_ref[...], c_ref[...], d_ref[...]], axis=-1).astype(F32)
    mixed = (mix * (g * jax.nn.sigmoid(g))).astype(BF16)
    x1 = x_ref[...] + jnp.dot(mixed, wo_ref[...], preferred_element_type=F32)
    h = _rms(x1, nw_ref[...]).astype(BF16)
    gate = jax.nn.sigmoid(jnp.dot(h, gw_ref[...], preferred_element_type=F32))
    pe = jnp.dot(p_ref[...].astype(BF16), pw_ref[...], preferred_element_type=F32)
    o_ref[...] = x1 + gate * pe


def out_ple(x, a, b, c, d, proj, p, w_out, ple_norm_w, ple_gate_w, ple_w, tm=256):
    s, dm = x.shape
    bw = a.shape[1]
    pd = p.shape[1]
    tm = min(tm, s)
    gb = OFF['ag'] // dm
    row = lambda w: pl.BlockSpec((tm, w), lambda i: (i, 0))
    const = lambda r, c_: pl.BlockSpec((r, c_), lambda i: (0, 0), pipeline_mode=pl.Buffered(1))
    return pl.pallas_call(
        _out_ple_kernel,
        grid=(s // tm,),
        in_specs=[row(dm), row(bw), row(bw), row(bw), row(bw),
                  pl.BlockSpec((tm, dm), lambda i: (i, gb)), row(pd),
                  const(dm, dm), const(1, dm), const(dm, dm), const(pd, dm)],
        out_specs=row(dm),
        out_shape=jax.ShapeDtypeStruct((s, dm), F32),
        compiler_params=_params("parallel"),
        name="out_ple",
    )(x, a, b, c, d, proj, p, w_out, ple_norm_w.reshape(1, dm), ple_gate_w, ple_w)


def _reorder_cols(w):
    return jnp.concatenate([w[:, _ORIG[n][0]:_ORIG[n][0] + _ORIG[n][1]] for n in _ORDER], axis=1)


def _pair_weights(ws, scale):
    return jnp.stack([jnp.tile(w.astype(F32) * scale, 2).reshape(1, LANES) for w in ws])


def kernel(x, p, norm_w, w_in, w_out, a_q_norm, a_k_norm, s5_lambda_re, s5_lambda_im, s5_log_dt, s5_b_re, s5_b_im, s5_c_re, s5_c_im, s5_d, s5_w_glu, s5_b_glu, c_q_norm, c_k_norm, d_q_norm, d_k_norm, d_lambda_q1, d_lambda_k1, d_lambda_q2, d_lambda_k2, d_subln, ple_norm_w, ple_gate_w, ple_w):
    bsz, s, dm = x.shape
    assert bsz == 1
    depth = w_in.shape[0]
    scale = HEAD_DIM ** -0.5
    scale2 = scale * math.log2(math.e)
    tab_1d = _rope_table_1d(s)
    tab_ax = _rope_table_axial(s)
    blk = lambda name, n: tuple(OFF[name] // LANES + j for j in range(n))
    xs = x[0]
    for i in range(depth):
        proj = in_proj(xs, norm_w[i], _reorder_cols(w_in[i].astype(BF16)))

        qk_a = qk_prep(proj, blk('aq', 4) + blk('ak', 4),
                       jnp.concatenate([_pair_weights([a_q_norm[i]] * 4, scale),
                                        _pair_weights([a_k_norm[i]] * 4, 1.0)]),
                       tab_1d, ROT_DIMS // 2, 'k', out_dtype=F32)
        q_d = qk_prep(proj, blk('dq', 4), _pair_weights([d_q_norm[i]] * 4, scale2), tab_1d,
                      ROT_DIMS // 2, 'q')
        k_d = qk_prep(proj, blk('dk', 4), _pair_weights([d_k_norm[i]] * 4, 1.0), tab_1d,
                      ROT_DIMS // 2, 'k')
        q1 = qk_prep(proj, blk('cq', 4), _pair_weights([c_q_norm[i]] * 4, scale2), tab_ax,
                     HEAD_DIM // 4, 'q')
        k1 = qk_prep(proj, blk('ck', 1), _pair_weights([c_k_norm[i]], 1.0), tab_ax, HEAD_DIM // 4, 'kdup')

        a_out = band_attn(qk_a, proj[:, OFF['av']:OFF['av'] + 512].astype(F32))

        b_out = mixer_b(proj, s5_lambda_re[i], s5_lambda_im[i], s5_log_dt[i], s5_b_re[i], s5_b_im[i],
                        s5_c_re[i], s5_c_im[i], s5_d[i], s5_w_glu[i], s5_b_glu[i])

        cv = proj[:, OFF['cv']:OFF['cv'] + LANES]
        v_dup = jnp.stack([jnp.tile(cv[:, :HEAD_DIM], (1, 2)), jnp.tile(cv[:, HEAD_DIM:], (1, 2))])
        c_out = gqa_attn(q1, k1, v_dup)

        lam_init = 0.8 - 0.6 * math.exp(-0.3 * i)
        pad = lambda v: jnp.pad(v.astype(F32), (0, LANES - HEAD_DIM))
        lam_rows = jnp.stack([pad(d_lambda_q1[i]), pad(d_lambda_k1[i]), pad(d_lambda_q2[i]),
                              pad(d_lambda_k2[i]), jnp.full((LANES,), lam_init, F32)]
                             + [jnp.zeros((LANES,), F32)] * 3)
        d_out = diff_attn(q_d, k_d, proj, lam_rows, d_subln[i])

        xs = out_ple(xs, a_out, b_out, c_out, d_out, proj, p[i, 0], w_out[i].astype(BF16),
                     ple_norm_w[i], ple_gate_w[i].astype(BF16), ple_w[i].astype(BF16))
    return xs[None]
```

```python
import functools
import math

import jax
import jax.numpy as jnp
import numpy as np
from jax import lax
from jax.experimental import pallas as pl
from jax.experimental.pallas import tpu as pltpu

F32 = jnp.float32
BF16 = jnp.bfloat16

HEAD_DIM = 64
LANES = 128
NORM_EPS = 1e-6
MASK_VALUE = -1e30
ROPE_THETA = 500000.0
AXIAL_THETA = 10000.0
ROT_DIMS = HEAD_DIM // 4
GRID_W = 64
DILATIONS = (1, 4, 16)
N_SIDE = 64
SSM_GROUP = 16
SSM_STATE = 64
N_SEG = 8
FLASH_ROWS = 64
FLASH_CHUNKS_PER_TRIP = 8
VMEM_LIMIT = 56 * 1024 * 1024

_ORIG = dict(aq=(0, 512), ak=(512, 512), av=(1024, 512), ag=(1536, 512), bu=(2048, 512),
             bg=(2560, 512), cq=(3072, 512), ck=(3584, 128), cv=(3712, 128), cg=(3840, 512),
             dq=(4352, 512), dk=(4864, 512), dv=(5376, 512), dg=(5888, 512))
_ORDER = ('aq', 'ak', 'dq', 'dk', 'cq', 'av', 'dv', 'bu', 'ag', 'bg', 'cg', 'dg', 'ck', 'cv')
OFF = {}
_o = 0
for _n in _ORDER:
    OFF[_n] = _o
    _o += _ORIG[_n][1]
IN_COLS = _o


def _params(*sem):
    return pltpu.CompilerParams(dimension_semantics=sem, vmem_limit_bytes=VMEM_LIMIT)


def _rms(x, w):
    return x * lax.rsqrt(jnp.mean(x * x, axis=-1, keepdims=True) + NORM_EPS) * w


def _in_proj_kernel(x_ref, nw_ref, w_ref, o_ref, h_ref):
    @pl.when(pl.program_id(1) == 0)
    def _():
        h_ref[...] = _rms(x_ref[...], nw_ref[...]).astype(BF16)

    o_ref[...] = jnp.dot(h_ref[...], w_ref[...], preferred_element_type=F32).astype(o_ref.dtype)


def in_proj(x, norm_w, w, tm=1024, tn=1280):
    s, d = x.shape
    n = w.shape[1]
    tm = min(tm, s)
    return pl.pallas_call(
        _in_proj_kernel,
        grid=(s // tm, n // tn),
        in_specs=[pl.BlockSpec((tm, d), lambda i, j: (i, 0)),
                  pl.BlockSpec((1, d), lambda i, j: (0, 0)),
                  pl.BlockSpec((d, tn), lambda i, j: (0, j))],
        out_specs=pl.BlockSpec((tm, tn), lambda i, j: (i, j)),
        out_shape=jax.ShapeDtypeStruct((s, n), BF16),
        scratch_shapes=[pltpu.VMEM((tm, d), BF16)],
        compiler_params=_params("parallel", "arbitrary"),
        name="in_proj",
    )(x, norm_w.reshape(1, d), w)


def _qk_prep_kernel(cb_ref, x_ref, w_ref, g_ref, t_ref, o_ref, *, shift, mode):
    del cb_ref
    x = x_ref[...].astype(F32)
    sq = x * x
    hi = sq.astype(BF16)
    lo = (sq - hi.astype(F32)).astype(BF16)
    g = g_ref[...]
    ms = jnp.dot(hi, g, preferred_element_type=F32) + jnp.dot(lo, g, preferred_element_type=F32)
    y = x * lax.rsqrt(ms + NORM_EPS) * w_ref[...]
    y = (y * t_ref[0] + pltpu.roll(y, shift, 1) * t_ref[1]
         + pltpu.roll(y, LANES - shift, 1) * t_ref[2])
    first = lax.broadcasted_iota(jnp.int32, y.shape, 1) < HEAD_DIM
    if mode == 'q':
        o_ref[0] = jnp.where(first, y, 0.0).astype(o_ref.dtype)
        o_ref[1] = jnp.where(first, 0.0, y).astype(o_ref.dtype)
    elif mode == 'k':
        o_ref[...] = y.astype(o_ref.dtype)
    else:
        r = pltpu.roll(y, HEAD_DIM, 1)
        o_ref[0] = jnp.where(first, y, r).astype(o_ref.dtype)
        o_ref[1] = jnp.where(first, r, y).astype(o_ref.dtype)


def qk_prep(proj, col_blocks, weights, table, shift, mode, out_dtype=BF16, tm=1024):
    s = proj.shape[0]
    tm = min(tm, s)
    nb = len(col_blocks)
    cb = jnp.asarray(col_blocks, jnp.int32)
    gmat = jnp.asarray(np.kron(np.eye(2), np.full((HEAD_DIM, HEAD_DIM), 1.0 / HEAD_DIM)), BF16)
    if mode == 'k':
        out_shape = jax.ShapeDtypeStruct((s, nb * LANES), out_dtype)
        out_spec = pl.BlockSpec((tm, LANES), lambda i, j, cb: (i, j))
    else:
        out_shape = jax.ShapeDtypeStruct((2 * nb, s, LANES), out_dtype)
        out_spec = pl.BlockSpec((2, tm, LANES), lambda i, j, cb: (j, i, 0))
    grid_spec = pltpu.PrefetchScalarGridSpec(
        num_scalar_prefetch=1,
        grid=(s // tm, nb),
        in_specs=[pl.BlockSpec((tm, LANES), lambda i, j, cb: (i, cb[j])),
                  pl.BlockSpec((None, 1, LANES), lambda i, j, cb: (j, 0, 0)),
                  pl.BlockSpec((LANES, LANES), lambda i, j, cb: (0, 0)),
                  pl.BlockSpec((3, tm, LANES), lambda i, j, cb: (0, i, 0))],
        out_specs=out_spec,
    )
    return pl.pallas_call(
        functools.partial(_qk_prep_kernel, shift=shift, mode=mode),
        grid_spec=grid_spec,
        out_shape=out_shape,
        compiler_params=_params("parallel", "arbitrary"),
        name="qk_prep_" + mode,
    )(cb, proj, weights, gmat, table)


def _rope_tables(pos, n_dims, theta):
    inv = theta ** (-jnp.arange(0, n_dims, 2, dtype=F32) / n_dims)
    ang = pos[:, None] * inv[None, :]
    return jnp.cos(ang), jnp.sin(ang)


def _rope_table_1d(s):
    cos, sin = _rope_tables(jnp.arange(s).astype(F32), ROT_DIMS, ROPE_THETA)
    z8 = jnp.zeros_like(cos)
    rest = HEAD_DIM - ROT_DIMS
    c = jnp.concatenate([cos, cos, jnp.ones((s, rest), F32)], axis=1)
    s1 = jnp.concatenate([z8, sin, jnp.zeros((s, rest), F32)], axis=1)
    s2 = jnp.concatenate([-sin, z8, jnp.zeros((s, rest), F32)], axis=1)
    return jnp.stack([jnp.tile(t, (1, 2)) for t in (c, s1, s2)])


def _rope_table_axial(s):
    t = jnp.arange(s)
    rows = s // GRID_W
    row_c = (t // GRID_W - rows // 2).astype(F32)
    col_c = (t % GRID_W - GRID_W // 2).astype(F32)
    cr, sr = _rope_tables(row_c, HEAD_DIM // 2, AXIAL_THETA)
    cc, sc = _rope_tables(col_c, HEAD_DIM // 2, AXIAL_THETA)
    z = jnp.zeros_like(cr)
    c = jnp.concatenate([cr, cr, cc, cc], axis=1)
    s1 = jnp.concatenate([z, sr, z, sc], axis=1)
    s2 = jnp.concatenate([-sr, z, -sc, z], axis=1)
    return jnp.stack([jnp.tile(t_, (1, 2)) for t_ in (c, s1, s2)])


def _band_attn_kernel(q_ref, k_ref, v_ref, o_ref, os_ref, ls_ref, *, tq, comb_rows):
    n_rows = k_ref.shape[0]
    span = q_ref.shape[0]
    win = tq + 2 * N_SIDE
    sp = pl.program_id(1)
    first = lax.broadcasted_iota(jnp.int32, (tq, LANES), 1) < HEAD_DIM
    row = lax.broadcasted_iota(jnp.int32, (2 * tq, win), 0)
    row = jnp.where(row >= tq, row - tq, row)
    rel = lax.broadcasted_iota(jnp.int32, (2 * tq, win), 1) - row

    for pi, d in enumerate(DILATIONS):
        seg_len = n_rows // d
        per_class = span // (tq * d)

        def tile(tid, carry, pi=pi, d=d, seg_len=seg_len, per_class=per_class):
            r = tid // per_class
            ti = tid % per_class
            i0 = sp * (span // d) + ti * tq
            c0 = jnp.clip(i0 - N_SIDE, 0, seg_len - win)
            q_rows = pl.ds(r + d * ti * tq, tq, stride=d)
            k_rows = pl.ds(r + d * c0, win, stride=d)
            q = q_ref[q_rows, :]
            q2 = jnp.concatenate([jnp.where(first, q, 0.0), jnp.where(first, 0.0, q)], axis=0)
            kw = k_ref[k_rows, :].astype(BF16)
            vw = v_ref[k_rows, :].astype(BF16)
            s = lax.dot_general(q2.astype(BF16), kw, (((1,), (1,)), ((), ())),
                                preferred_element_type=F32)
            s = jnp.where(jnp.abs(rel + (c0 - i0)) <= N_SIDE, s, MASK_VALUE)
            m = jnp.max(s, axis=-1, keepdims=True)
            p = jnp.exp(s - m)
            l = jnp.sum(p, axis=-1, keepdims=True)
            o = jnp.dot(p.astype(BF16), vw, preferred_element_type=F32) / l
            lse = m + jnp.log(l)
            os_ref[pi, q_rows, :] = jnp.where(first, o[:tq], o[tq:])
            ls_ref[pi, q_rows, :] = jnp.where(first, lse[:tq], lse[tq:])
            return carry

        lax.fori_loop(0, span // tq, tile, 0, unroll=4)

    def combine(c, carry):
        rows = pl.ds(pl.multiple_of(c * comb_rows, comb_rows), comb_rows)
        la, lb, lc = ls_ref[0, rows, :], ls_ref[1, rows, :], ls_ref[2, rows, :]
        m = jnp.maximum(jnp.maximum(la, lb), lc)
        ea, eb, ec = jnp.exp(la - m), jnp.exp(lb - m), jnp.exp(lc - m)
        num = ea * os_ref[0, rows, :] + eb * os_ref[1, rows, :] + ec * os_ref[2, rows, :]
        o_ref[rows, :] = (num / (ea + eb + ec)).astype(o_ref.dtype)
        return carry

    lax.fori_loop(0, span // comb_rows, combine, 0)


def band_attn(qk, v, tq=128, comb_rows=256):
    s = qk.shape[0]
    n_pairs = v.shape[1] // LANES
    span = tq * max(DILATIONS)
    assert s % span == 0 and s // max(DILATIONS) >= tq + 2 * N_SIDE
    return pl.pallas_call(
        functools.partial(_band_attn_kernel, tq=tq, comb_rows=comb_rows),
        grid=(n_pairs, s // span),
        in_specs=[pl.BlockSpec((span, LANES), lambda m, c: (c, m)),
                  pl.BlockSpec((s, LANES), lambda m, c: (0, n_pairs + m)),
                  pl.BlockSpec((s, LANES), lambda m, c: (0, m))],
        out_specs=pl.BlockSpec((span, LANES), lambda m, c: (c, m)),
        out_shape=jax.ShapeDtypeStruct((s, n_pairs * LANES), BF16),
        scratch_shapes=[pltpu.VMEM((len(DILATIONS), span, LANES), F32)] * 2,
        compiler_params=_params("parallel", "arbitrary"),
        name="band_attn",
    )(qk, qk, v)


def _s5_disc_kernel(lr_ref, li_ref, ldt_ref, br_ref, bi_ref, ar_ref, ai_ref, bbr_ref, bbi_ref):
    lr, li = lr_ref[...], li_ref[...]
    dt = jnp.exp(ldt_ref[...])
    mag = jnp.exp(lr * dt)
    ar = mag * jnp.cos(li * dt)
    ai = mag * jnp.sin(li * dt)
    den = lr * lr + li * li
    cre = ((ar - 1.0) * lr + ai * li) / den
    cim = (ai * lr - (ar - 1.0) * li) / den
    br, bi = br_ref[...], bi_ref[...]
    ar_ref[...] = ar
    ai_ref[...] = ai
    bbr_ref[...] = cre * br - cim * bi
    bbi_ref[...] = cre * bi + cim * br


def s5_discretise(lam_re, lam_im, log_dt, b_re, b_im):
    two, g, p = lam_re.shape
    c = b_re.shape[-1]
    rep = lambda t: jnp.repeat(t.reshape(two * g, p), c, axis=1)
    shp = jax.ShapeDtypeStruct((two * g, p * c), F32)
    ar, ai, bbr, bbi = pl.pallas_call(
        _s5_disc_kernel,
        out_shape=[shp] * 4,
        name="s5_disc",
    )(rep(lam_re), rep(lam_im), log_dt.reshape(two * g, 1),
      b_re.reshape(two * g, p * c), b_im.reshape(two * g, p * c))
    unrep = lambda t: t.reshape(two, g, p, c)[..., 0]
    return unrep(ar), unrep(ai), bbr.reshape(two, g, p, c), bbi.reshape(two, g, p, c)


def _block_diag(m):
    g, r, c = m.shape
    eye = jnp.eye(g, dtype=m.dtype)
    return jnp.einsum('grc,gh->grhc', m, eye).reshape(g * r, g * c)


def _s5_scan_kernel(u_ref, bre_ref, bim_ref, ar_ref, ai_ref, x0r_ref, x0i_ref, *rest,
                    emit, lane_blk):
    if emit:
        cre_ref, cim_ref, y_ref, xr_ref, xi_ref, sr_ref, si_ref = rest
    else:
        er_ref, ei_ref, xr_ref, xi_ref, sr_ref, si_ref = rest
    back = pl.program_id(0)
    ic = pl.program_id(1)
    n_i = u_ref.shape[0] // N_SEG
    n_state = xr_ref.shape[1]

    @pl.when(ic == 0)
    def _():
        sr_ref[...] = x0r_ref[...]
        si_ref[...] = x0i_ref[...]

    n_blk = n_state // lane_blk
    ch_blk = u_ref.shape[1] // n_blk

    def run(reverse):
        for lb in range(n_blk):
            cols = slice(lb * lane_blk, (lb + 1) * lane_blk)
            ch = slice(lb * ch_blk, (lb + 1) * ch_blk)
            u = u_ref[:, ch]
            xr_ref[:, cols] = jnp.dot(u, bre_ref[ch, cols], preferred_element_type=F32)
            xi_ref[:, cols] = jnp.dot(u, bim_ref[ch, cols], preferred_element_type=F32)
            ar = jnp.broadcast_to(ar_ref[:, cols], (N_SEG, lane_blk))
            ai = jnp.broadcast_to(ai_ref[:, cols], (N_SEG, lane_blk))
            xr, xi = sr_ref[:, cols], si_ref[:, cols]
            for i in (range(n_i - 1, -1, -1) if reverse else range(n_i)):
                rows = slice(i * N_SEG, (i + 1) * N_SEG)
                xr, xi = (ar * xr - ai * xi + xr_ref[rows, cols],
                          ar * xi + ai * xr + xi_ref[rows, cols])
                if emit:
                    xr_ref[rows, cols] = xr
                    xi_ref[rows, cols] = xi
            sr_ref[:, cols] = xr
            si_ref[:, cols] = xi
            if emit:
                y_ref[:, ch] = (
                    jnp.dot(xr_ref[:, cols].astype(BF16), cre_ref[cols, ch], preferred_element_type=F32)
                    - jnp.dot(xi_ref[:, cols].astype(BF16), cim_ref[cols, ch], preferred_element_type=F32))

    pl.when(back == 0)(lambda: run(False))
    pl.when(back == 1)(lambda: run(True))

    if not emit:
        @pl.when(ic == pl.num_programs(1) - 1)
        def _():
            er_ref[...] = sr_ref[...]
            ei_ref[...] = si_ref[...]


def s5_scan(u8, bmat_re, bmat_im, a_re, a_im, x0_re, x0_im, cmat_re=None, cmat_im=None,
            rows=256, lane_blk=512):
    s, w = u8.shape
    two = bmat_re.shape[0]
    n_state = bmat_re.shape[-1]
    rows = min(rows, s)
    nblk = s // rows
    emit = cmat_re is not None
    dspec = lambda shape: pl.BlockSpec((None,) + shape, lambda d, i: (d,) + (0,) * len(shape))
    blk = lambda d, i: i + d * (nblk - 1 - 2 * i)
    in_specs = [pl.BlockSpec((rows, w), lambda d, i: (blk(d, i), 0)),
                dspec((w, n_state)), dspec((w, n_state)),
                dspec((1, n_state)), dspec((1, n_state)),
                dspec((N_SEG, n_state)), dspec((N_SEG, n_state))]
    args = [u8, bmat_re, bmat_im, a_re, a_im, x0_re, x0_im]
    if emit:
        in_specs += [dspec((n_state, w)), dspec((n_state, w))]
        args += [cmat_re, cmat_im]
        out_specs = pl.BlockSpec((None, rows, w), lambda d, i: (d, blk(d, i), 0))
        out_shape = jax.ShapeDtypeStruct((two, s, w), F32)
    else:
        out_specs = [dspec((N_SEG, n_state))] * 2
        out_shape = [jax.ShapeDtypeStruct((two, N_SEG, n_state), F32)] * 2
    return pl.pallas_call(
        functools.partial(_s5_scan_kernel, emit=emit, lane_blk=lane_blk),
        grid=(two, s // rows),
        in_specs=in_specs,
        out_specs=out_specs,
        out_shape=out_shape,
        scratch_shapes=[pltpu.VMEM((rows, n_state), F32), pltpu.VMEM((rows, n_state), F32),
                        pltpu.VMEM((N_SEG, n_state), F32), pltpu.VMEM((N_SEG, n_state), F32)],
        compiler_params=_params("parallel", "arbitrary"),
        name="s5_scan_emit" if emit else "s5_scan_ends",
    )(*args)


def _s5_carry_kernel(er_ref, ei_ref, ar_ref, ai_ref, cr_ref, ci_ref, *, seg_len):
    assert seg_len & (seg_len - 1) == 0
    for d in range(er_ref.shape[0]):
        pr, pi = ar_ref[d], ai_ref[d]
        n = seg_len
        while n > 1:
            pr, pi = pr * pr - pi * pi, 2.0 * pr * pi
            n //= 2
        order = range(N_SEG) if d == 0 else range(N_SEG - 1, -1, -1)
        prev = None
        for j in order:
            if prev is None:
                cr = jnp.zeros_like(pr)
                ci = jnp.zeros_like(pi)
            else:
                cr, ci = (er_ref[d, prev:prev + 1, :] + pr * cr - pi * ci,
                          ei_ref[d, prev:prev + 1, :] + pr * ci + pi * cr)
            cr_ref[d, j:j + 1, :] = cr
            ci_ref[d, j:j + 1, :] = ci
            prev = j


def s5_carry(e_re, e_im, a_re, a_im, seg_len):
    shp = jax.ShapeDtypeStruct(e_re.shape, F32)
    return pl.pallas_call(
        functools.partial(_s5_carry_kernel, seg_len=seg_len),
        out_shape=[shp, shp],
        name="s5_carry",
    )(e_re, e_im, a_re, a_im)


def _s5_glu_kernel(u_ref, yf_ref, yb_ref, d_ref, w_ref, b_ref, o_ref):
    width = o_ref.shape[1]
    y = d_ref[...] * u_ref[...].astype(F32) + yf_ref[...] + yb_ref[...]
    c = math.sqrt(2.0 / math.pi)
    y = 0.5 * y * (1.0 + jnp.tanh(c * (y + 0.044715 * (y * y * y))))
    z = jnp.dot(y.astype(BF16), w_ref[...], preferred_element_type=F32) + b_ref[...]
    o_ref[...] = (z[:, :width] * jax.nn.sigmoid(z[:, width:])).astype(o_ref.dtype)


def s5_glu(u, y, d_skip, w_glu, b_glu, tm=1024):
    s, w = u.shape
    tm = min(tm, s)
    return pl.pallas_call(
        _s5_glu_kernel,
        grid=(s // tm,),
        in_specs=[pl.BlockSpec((tm, w), lambda i: (i, 0)),
                  pl.BlockSpec((None, tm, w), lambda i: (0, i, 0)),
                  pl.BlockSpec((None, tm, w), lambda i: (1, i, 0)),
                  pl.BlockSpec((1, w), lambda i: (0, 0)),
                  pl.BlockSpec((w, 2 * w), lambda i: (0, 0)),
                  pl.BlockSpec((1, 2 * w), lambda i: (0, 0))],
        out_specs=pl.BlockSpec((tm, w), lambda i: (i, 0)),
        out_shape=jax.ShapeDtypeStruct((s, w), BF16),
        compiler_params=_params("parallel"),
        name="s5_glu",
    )(u, y, y, d_skip.reshape(1, w), w_glu, b_glu.reshape(1, 2 * w))


def _to_segments(x):
    *lead, s, w = x.shape
    return x.reshape(*lead, N_SEG, s // N_SEG, w).swapaxes(-3, -2).reshape(*lead, s, w)


def _from_segments(x):
    *lead, s, w = x.shape
    return x.reshape(*lead, s // N_SEG, N_SEG, w).swapaxes(-3, -2).reshape(*lead, s, w)


def mixer_b(proj, lam_re, lam_im, log_dt, b_re, b_im, c_re, c_im, d_skip, w_glu, b_glu):
    s = proj.shape[0]
    two, g, p = lam_re.shape
    a_re, a_im, bb_re, bb_im = s5_discretise(lam_re, lam_im, log_dt, b_re, b_im)
    bmat_re = jnp.stack([_block_diag(bb_re[d].swapaxes(1, 2)) for d in range(two)]).astype(BF16)
    bmat_im = jnp.stack([_block_diag(bb_im[d].swapaxes(1, 2)) for d in range(two)]).astype(BF16)
    cmat_re = jnp.stack([_block_diag(c_re[d].swapaxes(1, 2)) for d in range(two)]).astype(BF16)
    cmat_im = jnp.stack([_block_diag(c_im[d].swapaxes(1, 2)) for d in range(two)]).astype(BF16)
    a_re = a_re.reshape(two, 1, g * p)
    a_im = a_im.reshape(two, 1, g * p)
    u8 = _to_segments(proj[:, OFF['bu']:OFF['bu'] + 512])
    zeros = jnp.zeros((two, N_SEG, g * p), F32)
    e_re, e_im = s5_scan(u8, bmat_re, bmat_im, a_re, a_im, zeros, zeros)
    x0_re, x0_im = s5_carry(e_re, e_im, a_re, a_im, s // N_SEG)
    y8 = s5_scan(u8, bmat_re, bmat_im, a_re, a_im, x0_re, x0_im, cmat_re, cmat_im)
    return _from_segments(s5_glu(u8, y8, d_skip, w_glu.astype(BF16), b_glu))


def _flash_scores(q_ref, k_ref, kc, tk, s_ref, slot):
    r = s_ref.shape[1]
    rows = pl.ds(pl.multiple_of(kc * tk, tk), tk)
    q = q_ref[...].reshape(r, LANES)
    s_ref[slot] = lax.dot_general(q, k_ref[rows, :], (((1,), (1,)), ((), ())),
                                  preferred_element_type=F32)


def _flash_update(v_ref, kc, tk, s_ref, slot, p_ref, m_ref, acc_ref):
    rows = pl.ds(pl.multiple_of(kc * tk, tk), tk)
    for rb in range(m_ref.shape[0] // FLASH_ROWS):
        rs = slice(rb * FLASH_ROWS, (rb + 1) * FLASH_ROWS)
        tiles = [s_ref[slot, rs, j * LANES:(j + 1) * LANES] for j in range(tk // LANES)]
        m_cur = jnp.max(functools.reduce(jnp.maximum, tiles), axis=-1, keepdims=True)
        m_prev = m_ref[rs]
        m_new = jnp.maximum(m_prev, jnp.broadcast_to(m_cur, m_prev.shape))
        alpha = jnp.exp2(m_prev - m_new)
        p_ref[rs] = jnp.concatenate([jnp.exp2((t - m_new).astype(BF16)) for t in tiles], axis=1)
        acc_ref[rs] = jnp.tile(alpha, (1, acc_ref.shape[1] // LANES)) * acc_ref[rs]
        m_ref[rs] = m_new
    acc_ref[...] += jnp.dot(p_ref[...], v_ref[rows, :], preferred_element_type=F32)


def _flash(q_ref, k_ref, v_ref, tk, s_ref, p_ref, m_ref, acc_ref):
    m_ref[...] = jnp.full(m_ref.shape, -jnp.inf, F32)
    acc_ref[...] = jnp.zeros(acc_ref.shape, F32)
    nk = k_ref.shape[0] // tk
    per_trip = min(FLASH_CHUNKS_PER_TRIP, nk)
    assert nk % per_trip == 0 and per_trip % 2 == 0
    _flash_scores(q_ref, k_ref, 0, tk, s_ref, 0)

    def body(j, carry):
        for c in range(per_trip):
            kc = per_trip * j + c
            nxt = kc + 1 if c + 1 < per_trip else jnp.minimum(kc + 1, nk - 1)
            _flash_scores(q_ref, k_ref, nxt, tk, s_ref, (c + 1) % 2)
            _flash_update(v_ref, kc, tk, s_ref, c % 2, p_ref, m_ref, acc_ref)
        return carry

    lax.fori_loop(0, nk // per_trip, body, 0)
    return acc_ref[...]


def _gqa_kernel(q_ref, k_ref, v_ref, o_ref, s_ref, p_ref, m_ref, acc_ref, *, tk):
    nh, tq, _ = q_ref.shape
    acc = _flash(q_ref, k_ref, v_ref, tk, s_ref, p_ref, m_ref, acc_ref)
    o = acc / pltpu.roll(acc, HEAD_DIM, 1)
    first = lax.broadcasted_iota(jnp.int32, (tq, LANES), 1) < HEAD_DIM
    for pair in range(nh // 2):
        even = o[(2 * pair) * tq:(2 * pair + 1) * tq]
        odd = pltpu.roll(o[(2 * pair + 1) * tq:(2 * pair + 2) * tq], HEAD_DIM, 1)
        o_ref[:, pair * LANES:(pair + 1) * LANES] = jnp.where(first, even, odd).astype(o_ref.dtype)


def gqa_attn(q_pad, k_dup, v_ones, tq=128, tk=1024):
    nh, s, _ = q_pad.shape
    nkv = k_dup.shape[0]
    grp = nh // nkv
    tq, tk = min(tq, s), min(tk, s)
    r = grp * tq
    return pl.pallas_call(
        functools.partial(_gqa_kernel, tk=tk),
        grid=(nkv, s // tq),
        in_specs=[pl.BlockSpec((grp, tq, LANES), lambda g, i: (g, i, 0)),
                  pl.BlockSpec((None, s, LANES), lambda g, i: (g, 0, 0)),
                  pl.BlockSpec((None, s, LANES), lambda g, i: (g, 0, 0))],
        out_specs=pl.BlockSpec((tq, grp * HEAD_DIM), lambda g, i: (i, g)),
        out_shape=jax.ShapeDtypeStruct((s, nh * HEAD_DIM), BF16),
        scratch_shapes=[pltpu.VMEM((2, r, tk), F32), pltpu.VMEM((r, tk), BF16)]
        + [pltpu.VMEM((r, LANES), F32)] * 2,
        compiler_params=_params("parallel", "arbitrary"),
        name="gqa_attn",
    )(q_pad, k_dup, v_ones)


def _diff_kernel(q_ref, k_ref, v_ref, lam_ref, w_ref, o_ref, s_ref, p_ref, m_ref, acc_ref, *, tk):
    _, tq, _ = q_ref.shape
    acc = _flash(q_ref, k_ref, v_ref, tk, s_ref, p_ref, m_ref, acc_ref)
    o = acc[:, :LANES] / acc[:, LANES:]
    lp = lam_ref[...]
    lam_init = lp[4:5, 0:1]
    lam = (jnp.exp(jnp.sum(lp[0:1] * lp[1:2], axis=-1, keepdims=True))
           - jnp.exp(jnp.sum(lp[2:3] * lp[3:4], axis=-1, keepdims=True)) + lam_init)
    d = o[:tq] - lam * o[tq:]
    o_ref[...] = (_rms(d, w_ref[...]) * (1.0 - lam_init)).astype(o_ref.dtype)


def diff_attn(q_pad, k, v_ones, lam_rows, subln_w, tq=256, tk=1024):
    n2, s, _ = q_pad.shape
    nh = n2 // 2
    tq, tk = min(tq, s), min(tk, s)
    return pl.pallas_call(
        functools.partial(_diff_kernel, tk=tk),
        grid=(nh, s // tq),
        in_specs=[pl.BlockSpec((2, tq, LANES), lambda h, i: (h, i, 0)),
                  pl.BlockSpec((s, LANES), lambda h, i: (0, h)),
                  pl.BlockSpec((s, 2 * LANES), lambda h, i: (0, h)),
                  pl.BlockSpec((8, LANES), lambda h, i: (0, 0)),
                  pl.BlockSpec((1, LANES), lambda h, i: (0, 0))],
        out_specs=pl.BlockSpec((tq, LANES), lambda h, i: (i, h)),
        out_shape=jax.ShapeDtypeStruct((s, nh * LANES), BF16),
        scratch_shapes=[pltpu.VMEM((2, 2 * tq, tk), F32), pltpu.VMEM((2 * tq, tk), BF16),
                        pltpu.VMEM((2 * tq, LANES), F32), pltpu.VMEM((2 * tq, 2 * LANES), F32)],
        compiler_params=_params("parallel", "arbitrary"),
        name="diff_attn",
    )(q_pad, k, v_ones, lam_rows, subln_w.reshape(1, LANES))


def _out_ple_kernel(x_ref, a_ref, b_ref, c_ref, d_ref, g_ref, p_ref, wo_ref, nw_ref, gw_ref, pw_ref,
                    o_ref):
    g = g_ref[...].astype(F32)
    mix = jnp.concatenate([a_ref[...], b_ref[...], c_ref[...], d_ref[...]], axis=-1).astype(F32)
    mixed = (mix * (g * jax.nn.sigmoid(g))).astype(BF16)
    x1 = x_ref[...] + jnp.dot(mixed, wo_ref[...], preferred_element_type=F32)
    h = _rms(x1, nw_ref[...]).astype(BF16)
    gate = jax.nn.sigmoid(jnp.dot(h, gw_ref[...], preferred_element_type=F32))
    pe = jnp.dot(p_ref[...].astype(BF16), pw_ref[...], preferred_element_type=F32)
    o_ref[...] = x1 + gate * pe


def out_ple(x, a, b, c, d, proj, p, w_out, ple_norm_w, ple_gate_w, ple_w, tm=256):
    s, dm = x.shape
    bw = a.shape[1]
    pd = p.shape[1]
    tm = min(tm, s)
    gb = OFF['ag'] // dm
    row = lambda w: pl.BlockSpec((tm, w), lambda i: (i, 0))
    const = lambda r, c_: pl.BlockSpec((r, c_), lambda i: (0, 0), pipeline_mode=pl.Buffered(1))
    return pl.pallas_call(
        _out_ple_kernel,
        grid=(s // tm,),
        in_specs=[row(dm), row(bw), row(bw), row(bw), row(bw),
                  pl.BlockSpec((tm, dm), lambda i: (i, gb)), row(pd),
                  const(dm, dm), const(1, dm), const(dm, dm), const(pd, dm)],
        out_specs=row(dm),
        out_shape=jax.ShapeDtypeStruct((s, dm), F32),
        compiler_params=_params("parallel"),
        name="out_ple",
    )(x, a, b, c, d, proj, p, w_out, ple_norm_w.reshape(1, dm), ple_gate_w, ple_w)


def _reorder_cols(w):
    return jnp.concatenate([w[:, _ORIG[n][0]:_ORIG[n][0] + _ORIG[n][1]] for n in _ORDER], axis=1)


def _pair_weights(ws, scale):
    return jnp.stack([jnp.tile(w.astype(F32) * scale, 2).reshape(1, LANES) for w in ws])


def kernel(x, p, norm_w, w_in, w_out, a_q_norm, a_k_norm, s5_lambda_re, s5_lambda_im, s5_log_dt, s5_b_re, s5_b_im, s5_c_re, s5_c_im, s5_d, s5_w_glu, s5_b_glu, c_q_norm, c_k_norm, d_q_norm, d_k_norm, d_lambda_q1, d_lambda_k1, d_lambda_q2, d_lambda_k2, d_subln, ple_norm_w, ple_gate_w, ple_w):
    bsz, s, dm = x.shape
    assert bsz == 1
    depth = w_in.shape[0]
    scale = HEAD_DIM ** -0.5
    scale2 = scale * math.log2(math.e)
    tab_1d = _rope_table_1d(s)
    tab_ax = _rope_table_axial(s)
    blk = lambda name, n: tuple(OFF[name] // LANES + j for j in range(n))
    xs = x[0]
    for i in range(depth):
        proj = in_proj(xs, norm_w[i], _reorder_cols(w_in[i].astype(BF16)))

        qk_a = qk_prep(proj, blk('aq', 4) + blk('ak', 4),
                       jnp.concatenate([_pair_weights([a_q_norm[i]] * 4, scale),
                                        _pair_weights([a_k_norm[i]] * 4, 1.0)]),
                       tab_1d, ROT_DIMS // 2, 'k', out_dtype=F32)
        q_d = qk_prep(proj, blk('dq', 4), _pair_weights([d_q_norm[i]] * 4, scale2), tab_1d,
                      ROT_DIMS // 2, 'q')
        k_d = qk_prep(proj, blk('dk', 4), _pair_weights([d_k_norm[i]] * 4, 1.0), tab_1d,
                      ROT_DIMS // 2, 'k')
        q1 = qk_prep(proj, blk('cq', 4), _pair_weights([c_q_norm[i]] * 4, scale2), tab_ax,
                     HEAD_DIM // 4, 'q')
        k1 = qk_prep(proj, blk('ck', 1), _pair_weights([c_k_norm[i]], 1.0), tab_ax, HEAD_DIM // 4, 'kdup')

        a_out = band_attn(qk_a, proj[:, OFF['av']:OFF['av'] + 512].astype(F32))

        b_out = mixer_b(proj, s5_lambda_re[i], s5_lambda_im[i], s5_log_dt[i], s5_b_re[i], s5_b_im[i],
                        s5_c_re[i], s5_c_im[i], s5_d[i], s5_w_glu[i], s5_b_glu[i])

        cv = proj[:, OFF['cv']:OFF['cv'] + LANES]
        ones = jnp.ones((s, HEAD_DIM), BF16)
        v_ones = jnp.stack([jnp.concatenate([cv[:, :HEAD_DIM], ones], axis=1),
                            jnp.concatenate([cv[:, HEAD_DIM:], ones], axis=1)])
        c_out = gqa_attn(q1, k1, v_ones)

        lam_init = 0.8 - 0.6 * math.exp(-0.3 * i)
        pad = lambda v: jnp.pad(v.astype(F32), (0, LANES - HEAD_DIM))
        lam_rows = jnp.stack([pad(d_lambda_q1[i]), pad(d_lambda_k1[i]), pad(d_lambda_q2[i]),
                              pad(d_lambda_k2[i]), jnp.full((LANES,), lam_init, F32)]
                             + [jnp.zeros((LANES,), F32)] * 3)
        dv = proj[:, OFF['dv']:OFF['dv'] + 512].reshape(s, 4, LANES)
        dv_ones = jnp.concatenate([dv, jnp.ones_like(dv)], axis=2).reshape(s, 8 * LANES)
        d_out = diff_attn(q_d, k_d, dv_ones, lam_rows, d_subln[i])

        xs = out_ple(xs, a_out, b_out, c_out, d_out, proj, p[i, 0], w_out[i].astype(BF16),
                     ple_norm_w[i], ple_gate_w[i].astype(BF16), ple_w[i].astype(BF16))
    return xs[None]
```

```python
import functools
import math

import jax
import jax.numpy as jnp
import numpy as np
from jax import lax
from jax.experimental import pallas as pl
from jax.experimental.pallas import tpu as pltpu

F32 = jnp.float32
BF16 = jnp.bfloat16

HEAD_DIM = 64
LANES = 128
NORM_EPS = 1e-6
MASK_VALUE = -1e30
ROPE_THETA = 500000.0
AXIAL_THETA = 10000.0
ROT_DIMS = HEAD_DIM // 4
GRID_W = 64
DILATIONS = (1, 4, 16)
N_SIDE = 64
SSM_GROUP = 16
SSM_STATE = 64
N_SEG = 8
FLASH_ROWS = 64
FLASH_CHUNKS_PER_TRIP = 8
VMEM_LIMIT = 56 * 1024 * 1024

_ORIG = dict(aq=(0, 512), ak=(512, 512), av=(1024, 512), ag=(1536, 512), bu=(2048, 512),
             bg=(2560, 512), cq=(3072, 512), ck=(3584, 128), cv=(3712, 128), cg=(3840, 512),
             dq=(4352, 512), dk=(4864, 512), dv=(5376, 512), dg=(5888, 512))
_ORDER = ('aq', 'ak', 'dq', 'dk', 'cq', 'av', 'dv', 'bu', 'ag', 'bg', 'cg', 'dg', 'ck', 'cv')
OFF = {}
_o = 0
for _n in _ORDER:
    OFF[_n] = _o
    _o += _ORIG[_n][1]
IN_COLS = _o


def _params(*sem):
    return pltpu.CompilerParams(dimension_semantics=sem, vmem_limit_bytes=VMEM_LIMIT)


def _rms(x, w):
    return x * lax.rsqrt(jnp.mean(x * x, axis=-1, keepdims=True) + NORM_EPS) * w


def _in_proj_kernel(x_ref, nw_ref, w_ref, o_ref, h_ref):
    @pl.when(pl.program_id(1) == 0)
    def _():
        h_ref[...] = _rms(x_ref[...], nw_ref[...]).astype(BF16)

    o_ref[...] = jnp.dot(h_ref[...], w_ref[...], preferred_element_type=F32).astype(o_ref.dtype)


def in_proj(x, norm_w, w, tm=1024, tn=1280):
    s, d = x.shape
    n = w.shape[1]
    tm = min(tm, s)
    return pl.pallas_call(
        _in_proj_kernel,
        grid=(s // tm, n // tn),
        in_specs=[pl.BlockSpec((tm, d), lambda i, j: (i, 0)),
                  pl.BlockSpec((1, d), lambda i, j: (0, 0)),
                  pl.BlockSpec((d, tn), lambda i, j: (0, j))],
        out_specs=pl.BlockSpec((tm, tn), lambda i, j: (i, j)),
        out_shape=jax.ShapeDtypeStruct((s, n), BF16),
        scratch_shapes=[pltpu.VMEM((tm, d), BF16)],
        compiler_params=_params("parallel", "arbitrary"),
        name="in_proj",
    )(x, norm_w.reshape(1, d), w)


def _qk_prep_kernel(cb_ref, x_ref, w_ref, g_ref, t_ref, o_ref, *, shift, mode):
    del cb_ref
    x = x_ref[...].astype(F32)
    ms = jnp.dot((x * x).astype(BF16), g_ref[...], preferred_element_type=F32)
    y = x * lax.rsqrt(ms + NORM_EPS) * w_ref[...]
    y = (y * t_ref[0] + pltpu.roll(y, shift, 1) * t_ref[1]
         + pltpu.roll(y, LANES - shift, 1) * t_ref[2])
    first = lax.broadcasted_iota(jnp.int32, y.shape, 1) < HEAD_DIM
    if mode == 'q':
        o_ref[0] = jnp.where(first, y, 0.0).astype(o_ref.dtype)
        o_ref[1] = jnp.where(first, 0.0, y).astype(o_ref.dtype)
    elif mode == 'k':
        o_ref[...] = y.astype(o_ref.dtype)
    else:
        r = pltpu.roll(y, HEAD_DIM, 1)
        o_ref[0] = jnp.where(first, y, r).astype(o_ref.dtype)
        o_ref[1] = jnp.where(first, r, y).astype(o_ref.dtype)


def qk_prep(proj, col_blocks, weights, table, shift, mode, out_dtype=BF16, tm=2048):
    s = proj.shape[0]
    tm = min(tm, s)
    nb = len(col_blocks)
    cb = jnp.asarray(col_blocks, jnp.int32)
    gmat = jnp.asarray(np.kron(np.eye(2), np.full((HEAD_DIM, HEAD_DIM), 1.0 / HEAD_DIM)), BF16)
    if mode == 'k':
        out_shape = jax.ShapeDtypeStruct((s, nb * LANES), out_dtype)
        out_spec = pl.BlockSpec((tm, LANES), lambda i, j, cb: (i, j))
    else:
        out_shape = jax.ShapeDtypeStruct((2 * nb, s, LANES), out_dtype)
        out_spec = pl.BlockSpec((2, tm, LANES), lambda i, j, cb: (j, i, 0))
    grid_spec = pltpu.PrefetchScalarGridSpec(
        num_scalar_prefetch=1,
        grid=(s // tm, nb),
        in_specs=[pl.BlockSpec((tm, LANES), lambda i, j, cb: (i, cb[j])),
                  pl.BlockSpec((None, 1, LANES), lambda i, j, cb: (j, 0, 0)),
                  pl.BlockSpec((LANES, LANES), lambda i, j, cb: (0, 0)),
                  pl.BlockSpec((3, tm, LANES), lambda i, j, cb: (0, i, 0))],
        out_specs=out_spec,
    )
    return pl.pallas_call(
        functools.partial(_qk_prep_kernel, shift=shift, mode=mode),
        grid_spec=grid_spec,
        out_shape=out_shape,
        compiler_params=_params("parallel", "arbitrary"),
        name="qk_prep_" + mode,
    )(cb, proj, weights, gmat, table)


def _rope_tables(pos, n_dims, theta):
    inv = theta ** (-jnp.arange(0, n_dims, 2, dtype=F32) / n_dims)
    ang = pos[:, None] * inv[None, :]
    return jnp.cos(ang), jnp.sin(ang)


def _rope_table_1d(s):
    cos, sin = _rope_tables(jnp.arange(s).astype(F32), ROT_DIMS, ROPE_THETA)
    z8 = jnp.zeros_like(cos)
    rest = HEAD_DIM - ROT_DIMS
    c = jnp.concatenate([cos, cos, jnp.ones((s, rest), F32)], axis=1)
    s1 = jnp.concatenate([z8, sin, jnp.zeros((s, rest), F32)], axis=1)
    s2 = jnp.concatenate([-sin, z8, jnp.zeros((s, rest), F32)], axis=1)
    return jnp.stack([jnp.tile(t, (1, 2)) for t in (c, s1, s2)])


def _rope_table_axial(s):
    t = jnp.arange(s)
    rows = s // GRID_W
    row_c = (t // GRID_W - rows // 2).astype(F32)
    col_c = (t % GRID_W - GRID_W // 2).astype(F32)
    cr, sr = _rope_tables(row_c, HEAD_DIM // 2, AXIAL_THETA)
    cc, sc = _rope_tables(col_c, HEAD_DIM // 2, AXIAL_THETA)
    z = jnp.zeros_like(cr)
    c = jnp.concatenate([cr, cr, cc, cc], axis=1)
    s1 = jnp.concatenate([z, sr, z, sc], axis=1)
    s2 = jnp.concatenate([-sr, z, -sc, z], axis=1)
    return jnp.stack([jnp.tile(t_, (1, 2)) for t_ in (c, s1, s2)])


def _band_attn_kernel(q_ref, k_ref, v_ref, o_ref, os_ref, ls_ref, bias_ref, *, tq, comb_rows):
    n_rows = k_ref.shape[0]
    span = q_ref.shape[0]
    win = tq + 2 * N_SIDE
    sp = pl.program_id(1)
    first = lax.broadcasted_iota(jnp.int32, (tq, LANES), 1) < HEAD_DIM
    row = lax.broadcasted_iota(jnp.int32, (2 * tq, win), 0)
    row = jnp.where(row >= tq, row - tq, row)
    rel = lax.broadcasted_iota(jnp.int32, (2 * tq, win), 1) - row
    for case in range(3):
        bias_ref[case] = jnp.where(jnp.abs(rel - case * N_SIDE) <= N_SIDE, 0.0, MASK_VALUE)

    for pi, d in enumerate(DILATIONS):
        seg_len = n_rows // d
        per_class = span // (tq * d)

        def tile(tid, carry, pi=pi, d=d, seg_len=seg_len, per_class=per_class):
            r = tid // per_class
            ti = tid % per_class
            i0 = sp * (span // d) + ti * tq
            c0 = jnp.clip(i0 - N_SIDE, 0, seg_len - win)
            q_rows = pl.ds(r + d * ti * tq, tq, stride=d)
            k_rows = pl.ds(r + d * c0, win, stride=d)
            q = q_ref[q_rows, :]
            q2 = jnp.concatenate([jnp.where(first, q, 0.0), jnp.where(first, 0.0, q)], axis=0)
            kw = k_ref[k_rows, :].astype(BF16)
            vw = v_ref[k_rows, :].astype(BF16)
            s = lax.dot_general(q2.astype(BF16), kw, (((1,), (1,)), ((), ())),
                                preferred_element_type=F32)
            s = s + bias_ref[(i0 - c0) // N_SIDE]
            m = jnp.max(s, axis=-1, keepdims=True)
            p = jnp.exp(s - m)
            l = jnp.sum(p, axis=-1, keepdims=True)
            o = jnp.dot(p.astype(BF16), vw, preferred_element_type=F32) / l
            lse = m + jnp.log(l)
            os_ref[pi, q_rows, :] = jnp.where(first, o[:tq], o[tq:])
            ls_ref[pi, q_rows, :] = jnp.where(first, lse[:tq], lse[tq:])
            return carry

        lax.fori_loop(0, span // tq, tile, 0, unroll=4)

    def combine(c, carry):
        rows = pl.ds(pl.multiple_of(c * comb_rows, comb_rows), comb_rows)
        la, lb, lc = ls_ref[0, rows, :], ls_ref[1, rows, :], ls_ref[2, rows, :]
        m = jnp.maximum(jnp.maximum(la, lb), lc)
        ea, eb, ec = jnp.exp(la - m), jnp.exp(lb - m), jnp.exp(lc - m)
        num = ea * os_ref[0, rows, :] + eb * os_ref[1, rows, :] + ec * os_ref[2, rows, :]
        o_ref[rows, :] = (num / (ea + eb + ec)).astype(o_ref.dtype)
        return carry

    lax.fori_loop(0, span // comb_rows, combine, 0)


def band_attn(qk, v, tq=128, comb_rows=256):
    s = qk.shape[0]
    n_pairs = v.shape[1] // LANES
    span = tq * max(DILATIONS)
    assert s % span == 0 and s // max(DILATIONS) >= tq + 2 * N_SIDE
    return pl.pallas_call(
        functools.partial(_band_attn_kernel, tq=tq, comb_rows=comb_rows),
        grid=(n_pairs, s // span),
        in_specs=[pl.BlockSpec((span, LANES), lambda m, c: (c, m)),
                  pl.BlockSpec((s, LANES), lambda m, c: (0, n_pairs + m)),
                  pl.BlockSpec((s, LANES), lambda m, c: (0, m))],
        out_specs=pl.BlockSpec((span, LANES), lambda m, c: (c, m)),
        out_shape=jax.ShapeDtypeStruct((s, n_pairs * LANES), BF16),
        scratch_shapes=[pltpu.VMEM((len(DILATIONS), span, LANES), F32)] * 2
        + [pltpu.VMEM((3, 2 * tq, tq + 2 * N_SIDE), F32)],
        compiler_params=_params("parallel", "arbitrary"),
        name="band_attn",
    )(qk, qk, v)


def _s5_disc_kernel(lr_ref, li_ref, ldt_ref, br_ref, bi_ref, ar_ref, ai_ref, bbr_ref, bbi_ref):
    lr, li = lr_ref[...], li_ref[...]
    dt = jnp.exp(ldt_ref[...])
    mag = jnp.exp(lr * dt)
    ar = mag * jnp.cos(li * dt)
    ai = mag * jnp.sin(li * dt)
    den = lr * lr + li * li
    cre = ((ar - 1.0) * lr + ai * li) / den
    cim = (ai * lr - (ar - 1.0) * li) / den
    br, bi = br_ref[...], bi_ref[...]
    ar_ref[...] = ar
    ai_ref[...] = ai
    bbr_ref[...] = cre * br - cim * bi
    bbi_ref[...] = cre * bi + cim * br


def s5_discretise(lam_re, lam_im, log_dt, b_re, b_im):
    two, g, p = lam_re.shape
    c = b_re.shape[-1]
    rep = lambda t: jnp.repeat(t.reshape(two * g, p), c, axis=1)
    shp = jax.ShapeDtypeStruct((two * g, p * c), F32)
    ar, ai, bbr, bbi = pl.pallas_call(
        _s5_disc_kernel,
        out_shape=[shp] * 4,
        name="s5_disc",
    )(rep(lam_re), rep(lam_im), log_dt.reshape(two * g, 1),
      b_re.reshape(two * g, p * c), b_im.reshape(two * g, p * c))
    unrep = lambda t: t.reshape(two, g, p, c)[..., 0]
    return unrep(ar), unrep(ai), bbr.reshape(two, g, p, c), bbi.reshape(two, g, p, c)


def _block_diag(m):
    g, r, c = m.shape
    eye = jnp.eye(g, dtype=m.dtype)
    return jnp.einsum('grc,gh->grhc', m, eye).reshape(g * r, g * c)


def _s5_scan_kernel(u_ref, bre_ref, bim_ref, ar_ref, ai_ref, x0r_ref, x0i_ref, *rest,
                    emit, lane_blk):
    if emit:
        cre_ref, cim_ref, y_ref, xr_ref, xi_ref, sr_ref, si_ref = rest
    else:
        er_ref, ei_ref, xr_ref, xi_ref, sr_ref, si_ref = rest
    back = pl.program_id(0)
    ic = pl.program_id(1)
    n_i = u_ref.shape[0] // N_SEG
    n_state = xr_ref.shape[1]

    @pl.when(ic == 0)
    def _():
        sr_ref[...] = x0r_ref[...]
        si_ref[...] = x0i_ref[...]

    n_blk = n_state // lane_blk
    ch_blk = u_ref.shape[1] // n_blk

    def run(reverse):
        for lb in range(n_blk):
            cols = slice(lb * lane_blk, (lb + 1) * lane_blk)
            ch = slice(lb * ch_blk, (lb + 1) * ch_blk)
            u = u_ref[:, ch]
            xr_ref[:, cols] = jnp.dot(u, bre_ref[ch, cols], preferred_element_type=F32)
            xi_ref[:, cols] = jnp.dot(u, bim_ref[ch, cols], preferred_element_type=F32)
            ar = jnp.broadcast_to(ar_ref[:, cols], (N_SEG, lane_blk))
            ai = jnp.broadcast_to(ai_ref[:, cols], (N_SEG, lane_blk))
            xr, xi = sr_ref[:, cols], si_ref[:, cols]
            for i in (range(n_i - 1, -1, -1) if reverse else range(n_i)):
                rows = slice(i * N_SEG, (i + 1) * N_SEG)
                xr, xi = (ar * xr - ai * xi + xr_ref[rows, cols],
                          ar * xi + ai * xr + xi_ref[rows, cols])
                if emit:
                    xr_ref[rows, cols] = xr
                    xi_ref[rows, cols] = xi
            sr_ref[:, cols] = xr
            si_ref[:, cols] = xi
            if emit:
                y_ref[:, ch] = (
                    jnp.dot(xr_ref[:, cols].astype(BF16), cre_ref[cols, ch], preferred_element_type=F32)
                    - jnp.dot(xi_ref[:, cols].astype(BF16), cim_ref[cols, ch], preferred_element_type=F32))

    pl.when(back == 0)(lambda: run(False))
    pl.when(back == 1)(lambda: run(True))

    if not emit:
        @pl.when(ic == pl.num_programs(1) - 1)
        def _():
            er_ref[...] = sr_ref[...]
            ei_ref[...] = si_ref[...]


def s5_scan(u8, bmat_re, bmat_im, a_re, a_im, x0_re, x0_im, cmat_re=None, cmat_im=None,
            rows=256, lane_blk=512):
    s, w = u8.shape
    two = bmat_re.shape[0]
    n_state = bmat_re.shape[-1]
    rows = min(rows, s)
    nblk = s // rows
    emit = cmat_re is not None
    dspec = lambda shape: pl.BlockSpec((None,) + shape, lambda d, i: (d,) + (0,) * len(shape))
    blk = lambda d, i: i + d * (nblk - 1 - 2 * i)
    in_specs = [pl.BlockSpec((rows, w), lambda d, i: (blk(d, i), 0)),
                dspec((w, n_state)), dspec((w, n_state)),
                dspec((1, n_state)), dspec((1, n_state)),
                dspec((N_SEG, n_state)), dspec((N_SEG, n_state))]
    args = [u8, bmat_re, bmat_im, a_re, a_im, x0_re, x0_im]
    if emit:
        in_specs += [dspec((n_state, w)), dspec((n_state, w))]
        args += [cmat_re, cmat_im]
        out_specs = pl.BlockSpec((None, rows, w), lambda d, i: (d, blk(d, i), 0))
        out_shape = jax.ShapeDtypeStruct((two, s, w), F32)
    else:
        out_specs = [dspec((N_SEG, n_state))] * 2
        out_shape = [jax.ShapeDtypeStruct((two, N_SEG, n_state), F32)] * 2
    return pl.pallas_call(
        functools.partial(_s5_scan_kernel, emit=emit, lane_blk=lane_blk),
        grid=(two, s // rows),
        in_specs=in_specs,
        out_specs=out_specs,
        out_shape=out_shape,
        scratch_shapes=[pltpu.VMEM((rows, n_state), F32), pltpu.VMEM((rows, n_state), F32),
                        pltpu.VMEM((N_SEG, n_state), F32), pltpu.VMEM((N_SEG, n_state), F32)],
        compiler_params=_params("parallel", "arbitrary"),
        name="s5_scan_emit" if emit else "s5_scan_ends",
    )(*args)


def _s5_carry_kernel(er_ref, ei_ref, ar_ref, ai_ref, cr_ref, ci_ref, *, seg_len):
    assert seg_len & (seg_len - 1) == 0
    for d in range(er_ref.shape[0]):
        pr, pi = ar_ref[d], ai_ref[d]
        n = seg_len
        while n > 1:
            pr, pi = pr * pr - pi * pi, 2.0 * pr * pi
            n //= 2
        order = range(N_SEG) if d == 0 else range(N_SEG - 1, -1, -1)
        prev = None
        for j in order:
            if prev is None:
                cr = jnp.zeros_like(pr)
                ci = jnp.zeros_like(pi)
            else:
                cr, ci = (er_ref[d, prev:prev + 1, :] + pr * cr - pi * ci,
                          ei_ref[d, prev:prev + 1, :] + pr * ci + pi * cr)
            cr_ref[d, j:j + 1, :] = cr
            ci_ref[d, j:j + 1, :] = ci
            prev = j


def s5_carry(e_re, e_im, a_re, a_im, seg_len):
    shp = jax.ShapeDtypeStruct(e_re.shape, F32)
    return pl.pallas_call(
        functools.partial(_s5_carry_kernel, seg_len=seg_len),
        out_shape=[shp, shp],
        name="s5_carry",
    )(e_re, e_im, a_re, a_im)


def _s5_glu_kernel(u_ref, yf_ref, yb_ref, d_ref, w_ref, b_ref, o_ref):
    width = o_ref.shape[1]
    y = d_ref[...] * u_ref[...].astype(F32) + yf_ref[...] + yb_ref[...]
    c = math.sqrt(2.0 / math.pi)
    y = 0.5 * y * (1.0 + jnp.tanh(c * (y + 0.044715 * (y * y * y))))
    z = jnp.dot(y.astype(BF16), w_ref[...], preferred_element_type=F32) + b_ref[...]
    o_ref[...] = (z[:, :width] * jax.nn.sigmoid(z[:, width:])).astype(o_ref.dtype)


def s5_glu(u, y, d_skip, w_glu, b_glu, tm=1024):
    s, w = u.shape
    tm = min(tm, s)
    return pl.pallas_call(
        _s5_glu_kernel,
        grid=(s // tm,),
        in_specs=[pl.BlockSpec((tm, w), lambda i: (i, 0)),
                  pl.BlockSpec((None, tm, w), lambda i: (0, i, 0)),
                  pl.BlockSpec((None, tm, w), lambda i: (1, i, 0)),
                  pl.BlockSpec((1, w), lambda i: (0, 0)),
                  pl.BlockSpec((w, 2 * w), lambda i: (0, 0)),
                  pl.BlockSpec((1, 2 * w), lambda i: (0, 0))],
        out_specs=pl.BlockSpec((tm, w), lambda i: (i, 0)),
        out_shape=jax.ShapeDtypeStruct((s, w), BF16),
        compiler_params=_params("parallel"),
        name="s5_glu",
    )(u, y, y, d_skip.reshape(1, w), w_glu, b_glu.reshape(1, 2 * w))


def _to_segments(x):
    *lead, s, w = x.shape
    return x.reshape(*lead, N_SEG, s // N_SEG, w).swapaxes(-3, -2).reshape(*lead, s, w)


def _from_segments(x):
    *lead, s, w = x.shape
    return x.reshape(*lead, s // N_SEG, N_SEG, w).swapaxes(-3, -2).reshape(*lead, s, w)


def mixer_b(proj, lam_re, lam_im, log_dt, b_re, b_im, c_re, c_im, d_skip, w_glu, b_glu):
    s = proj.shape[0]
    two, g, p = lam_re.shape
    a_re, a_im, bb_re, bb_im = s5_discretise(lam_re, lam_im, log_dt, b_re, b_im)
    bmat_re = jnp.stack([_block_diag(bb_re[d].swapaxes(1, 2)) for d in range(two)]).astype(BF16)
    bmat_im = jnp.stack([_block_diag(bb_im[d].swapaxes(1, 2)) for d in range(two)]).astype(BF16)
    cmat_re = jnp.stack([_block_diag(c_re[d].swapaxes(1, 2)) for d in range(two)]).astype(BF16)
    cmat_im = jnp.stack([_block_diag(c_im[d].swapaxes(1, 2)) for d in range(two)]).astype(BF16)
    a_re = a_re.reshape(two, 1, g * p)
    a_im = a_im.reshape(two, 1, g * p)
    u8 = _to_segments(proj[:, OFF['bu']:OFF['bu'] + 512])
    zeros = jnp.zeros((two, N_SEG, g * p), F32)
    e_re, e_im = s5_scan(u8, bmat_re, bmat_im, a_re, a_im, zeros, zeros)
    x0_re, x0_im = s5_carry(e_re, e_im, a_re, a_im, s // N_SEG)
    y8 = s5_scan(u8, bmat_re, bmat_im, a_re, a_im, x0_re, x0_im, cmat_re, cmat_im)
    return _from_segments(s5_glu(u8, y8, d_skip, w_glu.astype(BF16), b_glu))


def _flash_scores(q_ref, k_ref, kc, tk, s_ref, slot):
    r = s_ref.shape[1]
    rows = pl.ds(pl.multiple_of(kc * tk, tk), tk)
    q = q_ref[...].reshape(r, LANES)
    s_ref[slot] = lax.dot_general(q, k_ref[rows, :], (((1,), (1,)), ((), ())),
                                  preferred_element_type=F32)


def _flash_update(v_ref, kc, tk, s_ref, slot, p_ref, m_ref, acc_ref):
    rows = pl.ds(pl.multiple_of(kc * tk, tk), tk)
    for rb in range(m_ref.shape[0] // FLASH_ROWS):
        rs = slice(rb * FLASH_ROWS, (rb + 1) * FLASH_ROWS)
        tiles = [s_ref[slot, rs, j * LANES:(j + 1) * LANES] for j in range(tk // LANES)]
        m_cur = jnp.max(functools.reduce(jnp.maximum, tiles), axis=-1, keepdims=True)
        m_prev = m_ref[rs]
        m_new = jnp.maximum(m_prev, jnp.broadcast_to(m_cur, m_prev.shape))
        alpha = jnp.exp2(m_prev - m_new)
        p_ref[rs] = jnp.concatenate([jnp.exp2((t - m_new).astype(BF16)) for t in tiles], axis=1)
        acc_ref[rs] = jnp.tile(alpha, (1, acc_ref.shape[1] // LANES)) * acc_ref[rs]
        m_ref[rs] = m_new
    acc_ref[...] += jnp.dot(p_ref[...], v_ref[rows, :], preferred_element_type=F32)


def _flash(q_ref, k_ref, v_ref, tk, s_ref, p_ref, m_ref, acc_ref):
    m_ref[...] = jnp.full(m_ref.shape, -jnp.inf, F32)
    acc_ref[...] = jnp.zeros(acc_ref.shape, F32)
    nk = k_ref.shape[0] // tk
    per_trip = min(FLASH_CHUNKS_PER_TRIP, nk)
    assert nk % per_trip == 0 and per_trip % 2 == 0
    _flash_scores(q_ref, k_ref, 0, tk, s_ref, 0)

    def body(j, carry):
        for c in range(per_trip):
            kc = per_trip * j + c
            nxt = kc + 1 if c + 1 < per_trip else jnp.minimum(kc + 1, nk - 1)
            _flash_scores(q_ref, k_ref, nxt, tk, s_ref, (c + 1) % 2)
            _flash_update(v_ref, kc, tk, s_ref, c % 2, p_ref, m_ref, acc_ref)
        return carry

    lax.fori_loop(0, nk // per_trip, body, 0)
    return acc_ref[...]


def _gqa_kernel(q_ref, k_ref, v_ref, o_ref, s_ref, p_ref, m_ref, acc_ref, *, tk):
    nh, tq, _ = q_ref.shape
    acc = _flash(q_ref, k_ref, v_ref, tk, s_ref, p_ref, m_ref, acc_ref)
    o = acc / pltpu.roll(acc, HEAD_DIM, 1)
    first = lax.broadcasted_iota(jnp.int32, (tq, LANES), 1) < HEAD_DIM
    for pair in range(nh // 2):
        even = o[(2 * pair) * tq:(2 * pair + 1) * tq]
        odd = pltpu.roll(o[(2 * pair + 1) * tq:(2 * pair + 2) * tq], HEAD_DIM, 1)
        o_ref[:, pair * LANES:(pair + 1) * LANES] = jnp.where(first, even, odd).astype(o_ref.dtype)


def gqa_attn(q_pad, k_dup, v_ones, tq=256, tk=1024):
    nh, s, _ = q_pad.shape
    nkv = k_dup.shape[0]
    grp = nh // nkv
    tq, tk = min(tq, s), min(tk, s)
    r = grp * tq
    return pl.pallas_call(
        functools.partial(_gqa_kernel, tk=tk),
        grid=(nkv, s // tq),
        in_specs=[pl.BlockSpec((grp, tq, LANES), lambda g, i: (g, i, 0)),
                  pl.BlockSpec((None, s, LANES), lambda g, i: (g, 0, 0)),
                  pl.BlockSpec((None, s, LANES), lambda g, i: (g, 0, 0))],
        out_specs=pl.BlockSpec((tq, grp * HEAD_DIM), lambda g, i: (i, g)),
        out_shape=jax.ShapeDtypeStruct((s, nh * HEAD_DIM), BF16),
        scratch_shapes=[pltpu.VMEM((2, r, tk), F32), pltpu.VMEM((r, tk), BF16)]
        + [pltpu.VMEM((r, LANES), F32)] * 2,
        compiler_params=_params("parallel", "arbitrary"),
        name="gqa_attn",
    )(q_pad, k_dup, v_ones)


def _diff_kernel(q_ref, k_ref, v_ref, lam_ref, w_ref, o_ref, s_ref, p_ref, m_ref, acc_ref, *, tk):
    _, tq, _ = q_ref.shape
    acc = _flash(q_ref, k_ref, v_ref, tk, s_ref, p_ref, m_ref, acc_ref)
    o = acc[:, :LANES] / acc[:, LANES:]
    lp = lam_ref[...]
    lam_init = lp[4:5, 0:1]
    lam = (jnp.exp(jnp.sum(lp[0:1] * lp[1:2], axis=-1, keepdims=True))
           - jnp.exp(jnp.sum(lp[2:3] * lp[3:4], axis=-1, keepdims=True)) + lam_init)
    d = o[:tq] - lam * o[tq:]
    o_ref[...] = (_rms(d, w_ref[...]) * (1.0 - lam_init)).astype(o_ref.dtype)


def diff_attn(q_pad, k, v_ones, lam_rows, subln_w, tq=512, tk=1024):
    n2, s, _ = q_pad.shape
    nh = n2 // 2
    tq, tk = min(tq, s), min(tk, s)
    return pl.pallas_call(
        functools.partial(_diff_kernel, tk=tk),
        grid=(nh, s // tq),
        in_specs=[pl.BlockSpec((2, tq, LANES), lambda h, i: (h, i, 0)),
                  pl.BlockSpec((s, LANES), lambda h, i: (0, h)),
                  pl.BlockSpec((s, 2 * LANES), lambda h, i: (0, h)),
                  pl.BlockSpec((8, LANES), lambda h, i: (0, 0)),
                  pl.BlockSpec((1, LANES), lambda h, i: (0, 0))],
        out_specs=pl.BlockSpec((tq, LANES), lambda h, i: (i, h)),
        out_shape=jax.ShapeDtypeStruct((s, nh * LANES), BF16),
        scratch_shapes=[pltpu.VMEM((2, 2 * tq, tk), F32), pltpu.VMEM((2 * tq, tk), BF16),
                        pltpu.VMEM((2 * tq, LANES), F32), pltpu.VMEM((2 * tq, 2 * LANES), F32)],
        compiler_params=_params("parallel", "arbitrary"),
        name="diff_attn",
    )(q_pad, k, v_ones, lam_rows, subln_w.reshape(1, LANES))


def _out_ple_kernel(x_ref, a_ref, b_ref, c_ref, d_ref, g_ref, p_ref, wo_ref, nw_ref, gw_ref, pw_ref,
                    o_ref):
    g = g_ref[...].astype(F32)
    mix = jnp.concatenate([a_ref[...], b_ref[...], c_ref[...], d_ref[...]], axis=-1).astype(F32)
    mixed = (mix * (g * jax.nn.sigmoid(g))).astype(BF16)
    x1 = x_ref[...] + jnp.dot(mixed, wo_ref[...], preferred_element_type=F32)
    h = _rms(x1, nw_ref[...]).astype(BF16)
    gate = jax.nn.sigmoid(jnp.dot(h, gw_ref[...], preferred_element_type=F32))
    pe = jnp.dot(p_ref[...].astype(BF16), pw_ref[...], preferred_element_type=F32)
    o_ref[...] = x1 + gate * pe


def out_ple(x, a, b, c, d, proj, p, w_out, ple_norm_w, ple_gate_w, ple_w, tm=256):
    s, dm = x.shape
    bw = a.shape[1]
    pd = p.shape[1]
    tm = min(tm, s)
    gb = OFF['ag'] // dm
    row = lambda w: pl.BlockSpec((tm, w), lambda i: (i, 0))
    const = lambda r, c_: pl.BlockSpec((r, c_), lambda i: (0, 0), pipeline_mode=pl.Buffered(1))
    return pl.pallas_call(
        _out_ple_kernel,
        grid=(s // tm,),
        in_specs=[row(dm), row(bw), row(bw), row(bw), row(bw),
                  pl.BlockSpec((tm, dm), lambda i: (i, gb)), row(pd),
                  const(dm, dm), const(1, dm), const(dm, dm), const(pd, dm)],
        out_specs=row(dm),
        out_shape=jax.ShapeDtypeStruct((s, dm), F32),
        compiler_params=_params("parallel"),
        name="out_ple",
    )(x, a, b, c, d, proj, p, w_out, ple_norm_w.reshape(1, dm), ple_gate_w, ple_w)


def _reorder_cols(w):
    return jnp.concatenate([w[:, _ORIG[n][0]:_ORIG[n][0] + _ORIG[n][1]] for n in _ORDER], axis=1)


def _pair_weights(ws, scale):
    return jnp.stack([jnp.tile(w.astype(F32) * scale, 2).reshape(1, LANES) for w in ws])


def kernel(x, p, norm_w, w_in, w_out, a_q_norm, a_k_norm, s5_lambda_re, s5_lambda_im, s5_log_dt, s5_b_re, s5_b_im, s5_c_re, s5_c_im, s5_d, s5_w_glu, s5_b_glu, c_q_norm, c_k_norm, d_q_norm, d_k_norm, d_lambda_q1, d_lambda_k1, d_lambda_q2, d_lambda_k2, d_subln, ple_norm_w, ple_gate_w, ple_w):
    bsz, s, dm = x.shape
    assert bsz == 1
    depth = w_in.shape[0]
    scale = HEAD_DIM ** -0.5
    scale2 = scale * math.log2(math.e)
    tab_1d = _rope_table_1d(s)
    tab_ax = _rope_table_axial(s)
    blk = lambda name, n: tuple(OFF[name] // LANES + j for j in range(n))
    xs = x[0]
    for i in range(depth):
        proj = in_proj(xs, norm_w[i], _reorder_cols(w_in[i].astype(BF16)))

        qk_a = qk_prep(proj, blk('aq', 4) + blk('ak', 4),
                       jnp.concatenate([_pair_weights([a_q_norm[i]] * 4, scale),
                                        _pair_weights([a_k_norm[i]] * 4, 1.0)]),
                       tab_1d, ROT_DIMS // 2, 'k', out_dtype=F32)
        q_d = qk_prep(proj, blk('dq', 4), _pair_weights([d_q_norm[i]] * 4, scale2), tab_1d,
                      ROT_DIMS // 2, 'q')
        k_d = qk_prep(proj, blk('dk', 4), _pair_weights([d_k_norm[i]] * 4, 1.0), tab_1d,
                      ROT_DIMS // 2, 'k')
        q1 = qk_prep(proj, blk('cq', 4), _pair_weights([c_q_norm[i]] * 4, scale2), tab_ax,
                     HEAD_DIM // 4, 'q')
        k1 = qk_prep(proj, blk('ck', 1), _pair_weights([c_k_norm[i]], 1.0), tab_ax, HEAD_DIM // 4, 'kdup')

        a_out = band_attn(qk_a, proj[:, OFF['av']:OFF['av'] + 512].astype(F32))

        b_out = mixer_b(proj, s5_lambda_re[i], s5_lambda_im[i], s5_log_dt[i], s5_b_re[i], s5_b_im[i],
                        s5_c_re[i], s5_c_im[i], s5_d[i], s5_w_glu[i], s5_b_glu[i])

        cv = proj[:, OFF['cv']:OFF['cv'] + LANES]
        ones = jnp.ones((s, HEAD_DIM), BF16)
        v_ones = jnp.stack([jnp.concatenate([cv[:, :HEAD_DIM], ones], axis=1),
                            jnp.concatenate([cv[:, HEAD_DIM:], ones], axis=1)])
        c_out = gqa_attn(q1, k1, v_ones)

        lam_init = 0.8 - 0.6 * math.exp(-0.3 * i)
        pad = lambda v: jnp.pad(v.astype(F32), (0, LANES - HEAD_DIM))
        lam_rows = jnp.stack([pad(d_lambda_q1[i]), pad(d_lambda_k1[i]), pad(d_lambda_q2[i]),
                              pad(d_lambda_k2[i]), jnp.full((LANES,), lam_init, F32)]
                             + [jnp.zeros((LANES,), F32)] * 3)
        dv = proj[:, OFF['dv']:OFF['dv'] + 512].reshape(s, 4, LANES)
        dv_ones = jnp.concatenate([dv, jnp.ones_like(dv)], axis=2).reshape(s, 8 * LANES)
        d_out = diff_attn(q_d, k_d, dv_ones, lam_rows, d_subln[i])

        xs = out_ple(xs, a_out, b_out, c_out, d_out, proj, p[i, 0], w_out[i].astype(BF16),
                     ple_norm_w[i], ple_gate_w[i].astype(BF16), ple_w[i].astype(BF16))
    return xs[None]
```

```python
import functools
import math

import jax
import jax.numpy as jnp
import numpy as np
from jax import lax
from jax.experimental import pallas as pl
from jax.experimental.pallas import tpu as pltpu

F32 = jnp.float32
BF16 = jnp.bfloat16

HEAD_DIM = 64
LANES = 128
NORM_EPS = 1e-6
MASK_VALUE = -1e30
ROPE_THETA = 500000.0
AXIAL_THETA = 10000.0
ROT_DIMS = HEAD_DIM // 4
GRID_W = 64
DILATIONS = (1, 4, 16)
N_SIDE = 64
SSM_GROUP = 16
SSM_STATE = 64
N_SEG = 8
FLASH_ROWS = 64
FLASH_CHUNKS_PER_TRIP = 8
VMEM_LIMIT = 56 * 1024 * 1024

_ORIG = dict(aq=(0, 512), ak=(512, 512), av=(1024, 512), ag=(1536, 512), bu=(2048, 512),
             bg=(2560, 512), cq=(3072, 512), ck=(3584, 128), cv=(3712, 128), cg=(3840, 512),
             dq=(4352, 512), dk=(4864, 512), dv=(5376, 512), dg=(5888, 512))
_ORDER = ('aq', 'ak', 'dq', 'dk', 'cq', 'av', 'dv', 'bu', 'ag', 'bg', 'cg', 'dg', 'ck', 'cv')
OFF = {}
_o = 0
for _n in _ORDER:
    OFF[_n] = _o
    _o += _ORIG[_n][1]
IN_COLS = _o


def _params(*sem):
    return pltpu.CompilerParams(dimension_semantics=sem, vmem_limit_bytes=VMEM_LIMIT)


def _rms(x, w):
    return x * lax.rsqrt(jnp.mean(x * x, axis=-1, keepdims=True) + NORM_EPS) * w


def _in_proj_kernel(x_ref, nw_ref, w_ref, o_ref, h_ref):
    @pl.when(pl.program_id(1) == 0)
    def _():
        h_ref[...] = _rms(x_ref[...], nw_ref[...]).astype(BF16)

    o_ref[...] = jnp.dot(h_ref[...], w_ref[...], preferred_element_type=F32).astype(o_ref.dtype)


def in_proj(x, norm_w, w, tm=1024, tn=1280):
    s, d = x.shape
    n = w.shape[1]
    tm = min(tm, s)
    return pl.pallas_call(
        _in_proj_kernel,
        grid=(s // tm, n // tn),
        in_specs=[pl.BlockSpec((tm, d), lambda i, j: (i, 0)),
                  pl.BlockSpec((1, d), lambda i, j: (0, 0)),
                  pl.BlockSpec((d, tn), lambda i, j: (0, j))],
        out_specs=pl.BlockSpec((tm, tn), lambda i, j: (i, j)),
        out_shape=jax.ShapeDtypeStruct((s, n), BF16),
        scratch_shapes=[pltpu.VMEM((tm, d), BF16)],
        compiler_params=_params("parallel", "arbitrary"),
        name="in_proj",
    )(x, norm_w.reshape(1, d), w)


def _qk_prep_kernel(cb_ref, x_ref, w_ref, g_ref, t_ref, o_ref, *, shift, mode):
    del cb_ref
    x = x_ref[...].astype(F32)
    ms = jnp.dot((x * x).astype(BF16), g_ref[...], preferred_element_type=F32)
    y = x * lax.rsqrt(ms + NORM_EPS) * w_ref[...]
    y = (y * t_ref[0] + pltpu.roll(y, shift, 1) * t_ref[1]
         + pltpu.roll(y, LANES - shift, 1) * t_ref[2])
    first = lax.broadcasted_iota(jnp.int32, y.shape, 1) < HEAD_DIM
    if mode == 'q':
        o_ref[0] = jnp.where(first, y, 0.0).astype(o_ref.dtype)
        o_ref[1] = jnp.where(first, 0.0, y).astype(o_ref.dtype)
    elif mode == 'k':
        o_ref[...] = y.astype(o_ref.dtype)
    else:
        r = pltpu.roll(y, HEAD_DIM, 1)
        o_ref[0] = jnp.where(first, y, r).astype(o_ref.dtype)
        o_ref[1] = jnp.where(first, r, y).astype(o_ref.dtype)


def qk_prep(proj, col_blocks, weights, table, shift, mode, out_dtype=BF16, tm=2048):
    s = proj.shape[0]
    tm = min(tm, s)
    nb = len(col_blocks)
    cb = jnp.asarray(col_blocks, jnp.int32)
    gmat = jnp.asarray(np.kron(np.eye(2), np.full((HEAD_DIM, HEAD_DIM), 1.0 / HEAD_DIM)), BF16)
    if mode == 'k':
        out_shape = jax.ShapeDtypeStruct((s, nb * LANES), out_dtype)
        out_spec = pl.BlockSpec((tm, LANES), lambda i, j, cb: (i, j))
    else:
        out_shape = jax.ShapeDtypeStruct((2 * nb, s, LANES), out_dtype)
        out_spec = pl.BlockSpec((2, tm, LANES), lambda i, j, cb: (j, i, 0))
    grid_spec = pltpu.PrefetchScalarGridSpec(
        num_scalar_prefetch=1,
        grid=(s // tm, nb),
        in_specs=[pl.BlockSpec((tm, LANES), lambda i, j, cb: (i, cb[j])),
                  pl.BlockSpec((None, 1, LANES), lambda i, j, cb: (j, 0, 0)),
                  pl.BlockSpec((LANES, LANES), lambda i, j, cb: (0, 0)),
                  pl.BlockSpec((3, tm, LANES), lambda i, j, cb: (0, i, 0))],
        out_specs=out_spec,
    )
    return pl.pallas_call(
        functools.partial(_qk_prep_kernel, shift=shift, mode=mode),
        grid_spec=grid_spec,
        out_shape=out_shape,
        compiler_params=_params("parallel", "arbitrary"),
        name="qk_prep_" + mode,
    )(cb, proj, weights, gmat, table)


def _rope_tables(pos, n_dims, theta):
    inv = theta ** (-jnp.arange(0, n_dims, 2, dtype=F32) / n_dims)
    ang = pos[:, None] * inv[None, :]
    return jnp.cos(ang), jnp.sin(ang)


def _rope_table_1d(s):
    cos, sin = _rope_tables(jnp.arange(s).astype(F32), ROT_DIMS, ROPE_THETA)
    z8 = jnp.zeros_like(cos)
    rest = HEAD_DIM - ROT_DIMS
    c = jnp.concatenate([cos, cos, jnp.ones((s, rest), F32)], axis=1)
    s1 = jnp.concatenate([z8, sin, jnp.zeros((s, rest), F32)], axis=1)
    s2 = jnp.concatenate([-sin, z8, jnp.zeros((s, rest), F32)], axis=1)
    return jnp.stack([jnp.tile(t, (1, 2)) for t in (c, s1, s2)])


def _rope_table_axial(s):
    t = jnp.arange(s)
    rows = s // GRID_W
    row_c = (t // GRID_W - rows // 2).astype(F32)
    col_c = (t % GRID_W - GRID_W // 2).astype(F32)
    cr, sr = _rope_tables(row_c, HEAD_DIM // 2, AXIAL_THETA)
    cc, sc = _rope_tables(col_c, HEAD_DIM // 2, AXIAL_THETA)
    z = jnp.zeros_like(cr)
    c = jnp.concatenate([cr, cr, cc, cc], axis=1)
    s1 = jnp.concatenate([z, sr, z, sc], axis=1)
    s2 = jnp.concatenate([-sr, z, -sc, z], axis=1)
    return jnp.stack([jnp.tile(t_, (1, 2)) for t_ in (c, s1, s2)])


def _band_attn_kernel(q_ref, k_ref, v_ref, o_ref, os_ref, ls_ref, bias_ref, *, tq, comb_rows):
    n_rows = k_ref.shape[0]
    span = q_ref.shape[0]
    win = tq + 2 * N_SIDE
    sp = pl.program_id(1)
    first = lax.broadcasted_iota(jnp.int32, (tq, LANES), 1) < HEAD_DIM
    row = lax.broadcasted_iota(jnp.int32, (2 * tq, win), 0)
    row = jnp.where(row >= tq, row - tq, row)
    rel = lax.broadcasted_iota(jnp.int32, (2 * tq, win), 1) - row
    for case in range(3):
        bias_ref[case] = jnp.where(jnp.abs(rel - case * N_SIDE) <= N_SIDE, 0.0, MASK_VALUE)

    for pi, d in enumerate(DILATIONS):
        seg_len = n_rows // d
        per_class = span // (tq * d)

        def tile(tid, carry, pi=pi, d=d, seg_len=seg_len, per_class=per_class):
            r = tid // per_class
            ti = tid % per_class
            i0 = sp * (span // d) + ti * tq
            c0 = jnp.clip(i0 - N_SIDE, 0, seg_len - win)
            q_rows = pl.ds(r + d * ti * tq, tq, stride=d)
            k_rows = pl.ds(r + d * c0, win, stride=d)
            q = q_ref[q_rows, :]
            q2 = jnp.concatenate([jnp.where(first, q, 0.0), jnp.where(first, 0.0, q)], axis=0)
            kw = k_ref[k_rows, :].astype(BF16)
            vw = v_ref[k_rows, :].astype(BF16)
            s = lax.dot_general(q2.astype(BF16), kw, (((1,), (1,)), ((), ())),
                                preferred_element_type=F32)
            s = s + bias_ref[(i0 - c0) // N_SIDE]
            m = jnp.max(s, axis=-1, keepdims=True)
            p = jnp.exp(s - m)
            l = jnp.sum(p, axis=-1, keepdims=True)
            o = jnp.dot(p.astype(BF16), vw, preferred_element_type=F32) / l
            lse = m + jnp.log(l)
            os_ref[pi, q_rows, :] = jnp.where(first, o[:tq], o[tq:])
            ls_ref[pi, q_rows, :] = jnp.where(first, lse[:tq], lse[tq:])
            return carry

        lax.fori_loop(0, span // tq, tile, 0, unroll=8)

    def combine(c, carry):
        rows = pl.ds(pl.multiple_of(c * comb_rows, comb_rows), comb_rows)
        la, lb, lc = ls_ref[0, rows, :], ls_ref[1, rows, :], ls_ref[2, rows, :]
        m = jnp.maximum(jnp.maximum(la, lb), lc)
        ea, eb, ec = jnp.exp(la - m), jnp.exp(lb - m), jnp.exp(lc - m)
        num = ea * os_ref[0, rows, :] + eb * os_ref[1, rows, :] + ec * os_ref[2, rows, :]
        o_ref[rows, :] = (num / (ea + eb + ec)).astype(o_ref.dtype)
        return carry

    lax.fori_loop(0, span // comb_rows, combine, 0)


def band_attn(qk, v, tq=128, comb_rows=256):
    s = qk.shape[0]
    n_pairs = v.shape[1] // LANES
    span = tq * max(DILATIONS)
    assert s % span == 0 and s // max(DILATIONS) >= tq + 2 * N_SIDE
    return pl.pallas_call(
        functools.partial(_band_attn_kernel, tq=tq, comb_rows=comb_rows),
        grid=(n_pairs, s // span),
        in_specs=[pl.BlockSpec((span, LANES), lambda m, c: (c, m)),
                  pl.BlockSpec((s, LANES), lambda m, c: (0, n_pairs + m)),
                  pl.BlockSpec((s, LANES), lambda m, c: (0, m))],
        out_specs=pl.BlockSpec((span, LANES), lambda m, c: (c, m)),
        out_shape=jax.ShapeDtypeStruct((s, n_pairs * LANES), BF16),
        scratch_shapes=[pltpu.VMEM((len(DILATIONS), span, LANES), F32)] * 2
        + [pltpu.VMEM((3, 2 * tq, tq + 2 * N_SIDE), F32)],
        compiler_params=_params("parallel", "arbitrary"),
        name="band_attn",
    )(qk, qk, v)


def _s5_disc_kernel(lr_ref, li_ref, ldt_ref, br_ref, bi_ref, ar_ref, ai_ref, bbr_ref, bbi_ref):
    lr, li = lr_ref[...], li_ref[...]
    dt = jnp.exp(ldt_ref[...])
    mag = jnp.exp(lr * dt)
    ar = mag * jnp.cos(li * dt)
    ai = mag * jnp.sin(li * dt)
    den = lr * lr + li * li
    cre = ((ar - 1.0) * lr + ai * li) / den
    cim = (ai * lr - (ar - 1.0) * li) / den
    br, bi = br_ref[...], bi_ref[...]
    ar_ref[...] = ar
    ai_ref[...] = ai
    bbr_ref[...] = cre * br - cim * bi
    bbi_ref[...] = cre * bi + cim * br


def s5_discretise(lam_re, lam_im, log_dt, b_re, b_im):
    two, g, p = lam_re.shape
    c = b_re.shape[-1]
    rep = lambda t: jnp.repeat(t.reshape(two * g, p), c, axis=1)
    shp = jax.ShapeDtypeStruct((two * g, p * c), F32)
    ar, ai, bbr, bbi = pl.pallas_call(
        _s5_disc_kernel,
        out_shape=[shp] * 4,
        name="s5_disc",
    )(rep(lam_re), rep(lam_im), log_dt.reshape(two * g, 1),
      b_re.reshape(two * g, p * c), b_im.reshape(two * g, p * c))
    unrep = lambda t: t.reshape(two, g, p, c)[..., 0]
    return unrep(ar), unrep(ai), bbr.reshape(two, g, p, c), bbi.reshape(two, g, p, c)


def _block_diag(m):
    g, r, c = m.shape
    eye = jnp.eye(g, dtype=m.dtype)
    return jnp.einsum('grc,gh->grhc', m, eye).reshape(g * r, g * c)


def _s5_scan_kernel(u_ref, bre_ref, bim_ref, ar_ref, ai_ref, x0r_ref, x0i_ref, *rest,
                    emit, lane_blk):
    if emit:
        cre_ref, cim_ref, y_ref, xr_ref, xi_ref, sr_ref, si_ref = rest
    else:
        er_ref, ei_ref, xr_ref, xi_ref, sr_ref, si_ref = rest
    back = pl.program_id(0)
    ic = pl.program_id(1)
    n_i = u_ref.shape[0] // N_SEG
    n_state = xr_ref.shape[1]

    @pl.when(ic == 0)
    def _():
        sr_ref[...] = x0r_ref[...]
        si_ref[...] = x0i_ref[...]

    n_blk = n_state // lane_blk
    ch_blk = u_ref.shape[1] // n_blk

    def run(reverse):
        for lb in range(n_blk):
            cols = slice(lb * lane_blk, (lb + 1) * lane_blk)
            ch = slice(lb * ch_blk, (lb + 1) * ch_blk)
            u = u_ref[:, ch]
            xr_ref[:, cols] = jnp.dot(u, bre_ref[ch, cols], preferred_element_type=F32)
            xi_ref[:, cols] = jnp.dot(u, bim_ref[ch, cols], preferred_element_type=F32)
            ar = jnp.broadcast_to(ar_ref[:, cols], (N_SEG, lane_blk))
            ai = jnp.broadcast_to(ai_ref[:, cols], (N_SEG, lane_blk))
            xr, xi = sr_ref[:, cols], si_ref[:, cols]
            for i in (range(n_i - 1, -1, -1) if reverse else range(n_i)):
                rows = slice(i * N_SEG, (i + 1) * N_SEG)
                xr, xi = (ar * xr - ai * xi + xr_ref[rows, cols],
                          ar * xi + ai * xr + xi_ref[rows, cols])
                if emit:
                    xr_ref[rows, cols] = xr
                    xi_ref[rows, cols] = xi
            sr_ref[:, cols] = xr
            si_ref[:, cols] = xi
            if emit:
                y_ref[:, ch] = (
                    jnp.dot(xr_ref[:, cols].astype(BF16), cre_ref[cols, ch], preferred_element_type=F32)
                    - jnp.dot(xi_ref[:, cols].astype(BF16), cim_ref[cols, ch], preferred_element_type=F32))

    pl.when(back == 0)(lambda: run(False))
    pl.when(back == 1)(lambda: run(True))

    if not emit:
        @pl.when(ic == pl.num_programs(1) - 1)
        def _():
            er_ref[...] = sr_ref[...]
            ei_ref[...] = si_ref[...]


def s5_scan(u8, bmat_re, bmat_im, a_re, a_im, x0_re, x0_im, cmat_re=None, cmat_im=None,
            rows=256, lane_blk=512):
    s, w = u8.shape
    two = bmat_re.shape[0]
    n_state = bmat_re.shape[-1]
    rows = min(rows, s)
    nblk = s // rows
    emit = cmat_re is not None
    dspec = lambda shape: pl.BlockSpec((None,) + shape, lambda d, i: (d,) + (0,) * len(shape))
    blk = lambda d, i: i + d * (nblk - 1 - 2 * i)
    in_specs = [pl.BlockSpec((rows, w), lambda d, i: (blk(d, i), 0)),
                dspec((w, n_state)), dspec((w, n_state)),
                dspec((1, n_state)), dspec((1, n_state)),
                dspec((N_SEG, n_state)), dspec((N_SEG, n_state))]
    args = [u8, bmat_re, bmat_im, a_re, a_im, x0_re, x0_im]
    if emit:
        in_specs += [dspec((n_state, w)), dspec((n_state, w))]
        args += [cmat_re, cmat_im]
        out_specs = pl.BlockSpec((None, rows, w), lambda d, i: (d, blk(d, i), 0))
        out_shape = jax.ShapeDtypeStruct((two, s, w), F32)
    else:
        out_specs = [dspec((N_SEG, n_state))] * 2
        out_shape = [jax.ShapeDtypeStruct((two, N_SEG, n_state), F32)] * 2
    return pl.pallas_call(
        functools.partial(_s5_scan_kernel, emit=emit, lane_blk=lane_blk),
        grid=(two, s // rows),
        in_specs=in_specs,
        out_specs=out_specs,
        out_shape=out_shape,
        scratch_shapes=[pltpu.VMEM((rows, n_state), F32), pltpu.VMEM((rows, n_state), F32),
                        pltpu.VMEM((N_SEG, n_state), F32), pltpu.VMEM((N_SEG, n_state), F32)],
        compiler_params=_params("parallel", "arbitrary"),
        name="s5_scan_emit" if emit else "s5_scan_ends",
    )(*args)


def _s5_carry_kernel(er_ref, ei_ref, ar_ref, ai_ref, cr_ref, ci_ref, *, seg_len):
    assert seg_len & (seg_len - 1) == 0
    for d in range(er_ref.shape[0]):
        pr, pi = ar_ref[d], ai_ref[d]
        n = seg_len
        while n > 1:
            pr, pi = pr * pr - pi * pi, 2.0 * pr * pi
            n //= 2
        order = range(N_SEG) if d == 0 else range(N_SEG - 1, -1, -1)
        prev = None
        for j in order:
            if prev is None:
                cr = jnp.zeros_like(pr)
                ci = jnp.zeros_like(pi)
            else:
                cr, ci = (er_ref[d, prev:prev + 1, :] + pr * cr - pi * ci,
                          ei_ref[d, prev:prev + 1, :] + pr * ci + pi * cr)
            cr_ref[d, j:j + 1, :] = cr
            ci_ref[d, j:j + 1, :] = ci
            prev = j


def s5_carry(e_re, e_im, a_re, a_im, seg_len):
    shp = jax.ShapeDtypeStruct(e_re.shape, F32)
    return pl.pallas_call(
        functools.partial(_s5_carry_kernel, seg_len=seg_len),
        out_shape=[shp, shp],
        name="s5_carry",
    )(e_re, e_im, a_re, a_im)


def _s5_glu_kernel(u_ref, yf_ref, yb_ref, d_ref, w_ref, b_ref, o_ref):
    width = o_ref.shape[1]
    y = d_ref[...] * u_ref[...].astype(F32) + yf_ref[...] + yb_ref[...]
    c = math.sqrt(2.0 / math.pi)
    y = 0.5 * y * (1.0 + jnp.tanh(c * (y + 0.044715 * (y * y * y))))
    z = jnp.dot(y.astype(BF16), w_ref[...], preferred_element_type=F32) + b_ref[...]
    o_ref[...] = (z[:, :width] * jax.nn.sigmoid(z[:, width:])).astype(o_ref.dtype)


def s5_glu(u, y, d_skip, w_glu, b_glu, tm=1024):
    s, w = u.shape
    tm = min(tm, s)
    return pl.pallas_call(
        _s5_glu_kernel,
        grid=(s // tm,),
        in_specs=[pl.BlockSpec((tm, w), lambda i: (i, 0)),
                  pl.BlockSpec((None, tm, w), lambda i: (0, i, 0)),
                  pl.BlockSpec((None, tm, w), lambda i: (1, i, 0)),
                  pl.BlockSpec((1, w), lambda i: (0, 0)),
                  pl.BlockSpec((w, 2 * w), lambda i: (0, 0)),
                  pl.BlockSpec((1, 2 * w), lambda i: (0, 0))],
        out_specs=pl.BlockSpec((tm, w), lambda i: (i, 0)),
        out_shape=jax.ShapeDtypeStruct((s, w), BF16),
        compiler_params=_params("parallel"),
        name="s5_glu",
    )(u, y, y, d_skip.reshape(1, w), w_glu, b_glu.reshape(1, 2 * w))


def _to_segments(x):
    *lead, s, w = x.shape
    return x.reshape(*lead, N_SEG, s // N_SEG, w).swapaxes(-3, -2).reshape(*lead, s, w)


def _from_segments(x):
    *lead, s, w = x.shape
    return x.reshape(*lead, s // N_SEG, N_SEG, w).swapaxes(-3, -2).reshape(*lead, s, w)


def mixer_b(proj, lam_re, lam_im, log_dt, b_re, b_im, c_re, c_im, d_skip, w_glu, b_glu):
    s = proj.shape[0]
    two, g, p = lam_re.shape
    a_re, a_im, bb_re, bb_im = s5_discretise(lam_re, lam_im, log_dt, b_re, b_im)
    bmat_re = jnp.stack([_block_diag(bb_re[d].swapaxes(1, 2)) for d in range(two)]).astype(BF16)
    bmat_im = jnp.stack([_block_diag(bb_im[d].swapaxes(1, 2)) for d in range(two)]).astype(BF16)
    cmat_re = jnp.stack([_block_diag(c_re[d].swapaxes(1, 2)) for d in range(two)]).astype(BF16)
    cmat_im = jnp.stack([_block_diag(c_im[d].swapaxes(1, 2)) for d in range(two)]).astype(BF16)
    a_re = a_re.reshape(two, 1, g * p)
    a_im = a_im.reshape(two, 1, g * p)
    u8 = _to_segments(proj[:, OFF['bu']:OFF['bu'] + 512])
    zeros = jnp.zeros((two, N_SEG, g * p), F32)
    e_re, e_im = s5_scan(u8, bmat_re, bmat_im, a_re, a_im, zeros, zeros)
    x0_re, x0_im = s5_carry(e_re, e_im, a_re, a_im, s // N_SEG)
    y8 = s5_scan(u8, bmat_re, bmat_im, a_re, a_im, x0_re, x0_im, cmat_re, cmat_im)
    return _from_segments(s5_glu(u8, y8, d_skip, w_glu.astype(BF16), b_glu))


def _flash_scores(q_ref, k_ref, kc, tk, s_ref, slot):
    r = s_ref.shape[1]
    rows = pl.ds(pl.multiple_of(kc * tk, tk), tk)
    q = q_ref[...].reshape(r, LANES)
    s_ref[slot] = lax.dot_general(q, k_ref[rows, :], (((1,), (1,)), ((), ())),
                                  preferred_element_type=F32)


def _flash_update(v_ref, kc, tk, s_ref, slot, p_ref, m_ref, acc_ref):
    rows = pl.ds(pl.multiple_of(kc * tk, tk), tk)
    for rb in range(m_ref.shape[0] // FLASH_ROWS):
        rs = slice(rb * FLASH_ROWS, (rb + 1) * FLASH_ROWS)
        tiles = [s_ref[slot, rs, j * LANES:(j + 1) * LANES] for j in range(tk // LANES)]
        m_cur = jnp.max(functools.reduce(jnp.maximum, tiles), axis=-1, keepdims=True)
        m_prev = m_ref[rs]
        m_new = jnp.maximum(m_prev, jnp.broadcast_to(m_cur, m_prev.shape))
        alpha = jnp.exp2(m_prev - m_new)
        p_ref[rs] = jnp.concatenate([jnp.exp2((t - m_new).astype(BF16)) for t in tiles], axis=1)
        acc_ref[rs] = jnp.tile(alpha, (1, acc_ref.shape[1] // LANES)) * acc_ref[rs]
        m_ref[rs] = m_new
    acc_ref[...] += jnp.dot(p_ref[...], v_ref[rows, :], preferred_element_type=F32)


def _flash(q_ref, k_ref, v_ref, tk, s_ref, p_ref, m_ref, acc_ref):
    m_ref[...] = jnp.full(m_ref.shape, -jnp.inf, F32)
    acc_ref[...] = jnp.zeros(acc_ref.shape, F32)
    nk = k_ref.shape[0] // tk
    per_trip = min(FLASH_CHUNKS_PER_TRIP, nk)
    assert nk % per_trip == 0 and per_trip % 2 == 0
    _flash_scores(q_ref, k_ref, 0, tk, s_ref, 0)

    def body(j, carry):
        for c in range(per_trip):
            kc = per_trip * j + c
            nxt = kc + 1 if c + 1 < per_trip else jnp.minimum(kc + 1, nk - 1)
            _flash_scores(q_ref, k_ref, nxt, tk, s_ref, (c + 1) % 2)
            _flash_update(v_ref, kc, tk, s_ref, c % 2, p_ref, m_ref, acc_ref)
        return carry

    lax.fori_loop(0, nk // per_trip, body, 0)
    return acc_ref[...]


def _gqa_kernel(q_ref, k_ref, v_ref, o_ref, s_ref, p_ref, m_ref, acc_ref, *, tk):
    nh, tq, _ = q_ref.shape
    acc = _flash(q_ref, k_ref, v_ref, tk, s_ref, p_ref, m_ref, acc_ref)
    o = acc / pltpu.roll(acc, HEAD_DIM, 1)
    first = lax.broadcasted_iota(jnp.int32, (tq, LANES), 1) < HEAD_DIM
    for pair in range(nh // 2):
        even = o[(2 * pair) * tq:(2 * pair + 1) * tq]
        odd = pltpu.roll(o[(2 * pair + 1) * tq:(2 * pair + 2) * tq], HEAD_DIM, 1)
        o_ref[:, pair * LANES:(pair + 1) * LANES] = jnp.where(first, even, odd).astype(o_ref.dtype)


def gqa_attn(q_pad, k_dup, v_ones, tq=256, tk=1024):
    nh, s, _ = q_pad.shape
    nkv = k_dup.shape[0]
    grp = nh // nkv
    tq, tk = min(tq, s), min(tk, s)
    r = grp * tq
    return pl.pallas_call(
        functools.partial(_gqa_kernel, tk=tk),
        grid=(nkv, s // tq),
        in_specs=[pl.BlockSpec((grp, tq, LANES), lambda g, i: (g, i, 0)),
                  pl.BlockSpec((None, s, LANES), lambda g, i: (g, 0, 0)),
                  pl.BlockSpec((None, s, LANES), lambda g, i: (g, 0, 0))],
        out_specs=pl.BlockSpec((tq, grp * HEAD_DIM), lambda g, i: (i, g)),
        out_shape=jax.ShapeDtypeStruct((s, nh * HEAD_DIM), BF16),
        scratch_shapes=[pltpu.VMEM((2, r, tk), F32), pltpu.VMEM((r, tk), BF16)]
        + [pltpu.VMEM((r, LANES), F32)] * 2,
        compiler_params=_params("parallel", "arbitrary"),
        name="gqa_attn",
    )(q_pad, k_dup, v_ones)


def _diff_kernel(q_ref, k_ref, v_ref, lam_ref, w_ref, o_ref, s_ref, p_ref, m_ref, acc_ref, *, tk):
    _, tq, _ = q_ref.shape
    acc = _flash(q_ref, k_ref, v_ref, tk, s_ref, p_ref, m_ref, acc_ref)
    o = acc[:, :LANES] / acc[:, LANES:]
    lp = lam_ref[...]
    lam_init = lp[4:5, 0:1]
    lam = (jnp.exp(jnp.sum(lp[0:1] * lp[1:2], axis=-1, keepdims=True))
           - jnp.exp(jnp.sum(lp[2:3] * lp[3:4], axis=-1, keepdims=True)) + lam_init)
    d = o[:tq] - lam * o[tq:]
    o_ref[...] = (_rms(d, w_ref[...]) * (1.0 - lam_init)).astype(o_ref.dtype)


def diff_attn(q_pad, k, v_ones, lam_rows, subln_w, tq=512, tk=1024):
    n2, s, _ = q_pad.shape
    nh = n2 // 2
    tq, tk = min(tq, s), min(tk, s)
    return pl.pallas_call(
        functools.partial(_diff_kernel, tk=tk),
        grid=(nh, s // tq),
        in_specs=[pl.BlockSpec((2, tq, LANES), lambda h, i: (h, i, 0)),
                  pl.BlockSpec((s, LANES), lambda h, i: (0, h)),
                  pl.BlockSpec((s, 2 * LANES), lambda h, i: (0, h)),
                  pl.BlockSpec((8, LANES), lambda h, i: (0, 0)),
                  pl.BlockSpec((1, LANES), lambda h, i: (0, 0))],
        out_specs=pl.BlockSpec((tq, LANES), lambda h, i: (i, h)),
        out_shape=jax.ShapeDtypeStruct((s, nh * LANES), BF16),
        scratch_shapes=[pltpu.VMEM((2, 2 * tq, tk), F32), pltpu.VMEM((2 * tq, tk), BF16),
                        pltpu.VMEM((2 * tq, LANES), F32), pltpu.VMEM((2 * tq, 2 * LANES), F32)],
        compiler_params=_params("parallel", "arbitrary"),
        name="diff_attn",
    )(q_pad, k, v_ones, lam_rows, subln_w.reshape(1, LANES))


def _out_ple_kernel(x_ref, a_ref, b_ref, c_ref, d_ref, g_ref, p_ref, wo_ref, nw_ref, gw_ref, pw_ref,
                    o_ref):
    g = g_ref[...].astype(F32)
    mix = jnp.concatenate([a_ref[...], b_ref[...], c_ref[...], d_ref[...]], axis=-1).astype(F32)
    mixed = (mix * (g * jax.nn.sigmoid(g))).astype(BF16)
    x1 = x_ref[...] + jnp.dot(mixed, wo_ref[...], preferred_element_type=F32)
    h = _rms(x1, nw_ref[...]).astype(BF16)
    gate = jax.nn.sigmoid(jnp.dot(h, gw_ref[...], preferred_element_type=F32))
    pe = jnp.dot(p_ref[...].astype(BF16), pw_ref[...], preferred_element_type=F32)
    o_ref[...] = x1 + gate * pe


def out_ple(x, a, b, c, d, proj, p, w_out, ple_norm_w, ple_gate_w, ple_w, tm=256):
    s, dm = x.shape
    bw = a.shape[1]
    pd = p.shape[1]
    tm = min(tm, s)
    gb = OFF['ag'] // dm
    row = lambda w: pl.BlockSpec((tm, w), lambda i: (i, 0))
    const = lambda r, c_: pl.BlockSpec((r, c_), lambda i: (0, 0), pipeline_mode=pl.Buffered(1))
    return pl.pallas_call(
        _out_ple_kernel,
        grid=(s // tm,),
        in_specs=[row(dm), row(bw), row(bw), row(bw), row(bw),
                  pl.BlockSpec((tm, dm), lambda i: (i, gb)), row(pd),
                  const(dm, dm), const(1, dm), const(dm, dm), const(pd, dm)],
        out_specs=row(dm),
        out_shape=jax.ShapeDtypeStruct((s, dm), F32),
        compiler_params=_params("parallel"),
        name="out_ple",
    )(x, a, b, c, d, proj, p, w_out, ple_norm_w.reshape(1, dm), ple_gate_w, ple_w)


def _reorder_cols(w):
    return jnp.concatenate([w[:, _ORIG[n][0]:_ORIG[n][0] + _ORIG[n][1]] for n in _ORDER], axis=1)


def _pair_weights(ws, scale):
    return jnp.stack([jnp.tile(w.astype(F32) * scale, 2).reshape(1, LANES) for w in ws])


def kernel(x, p, norm_w, w_in, w_out, a_q_norm, a_k_norm, s5_lambda_re, s5_lambda_im, s5_log_dt, s5_b_re, s5_b_im, s5_c_re, s5_c_im, s5_d, s5_w_glu, s5_b_glu, c_q_norm, c_k_norm, d_q_norm, d_k_norm, d_lambda_q1, d_lambda_k1, d_lambda_q2, d_lambda_k2, d_subln, ple_norm_w, ple_gate_w, ple_w):
    bsz, s, dm = x.shape
    assert bsz == 1
    depth = w_in.shape[0]
    scale = HEAD_DIM ** -0.5
    scale2 = scale * math.log2(math.e)
    tab_1d = _rope_table_1d(s)
    tab_ax = _rope_table_axial(s)
    blk = lambda name, n: tuple(OFF[name] // LANES + j for j in range(n))
    xs = x[0]
    for i in range(depth):
        proj = in_proj(xs, norm_w[i], _reorder_cols(w_in[i].astype(BF16)))

        qk_a = qk_prep(proj, blk('aq', 4) + blk('ak', 4),
                       jnp.concatenate([_pair_weights([a_q_norm[i]] * 4, scale),
                                        _pair_weights([a_k_norm[i]] * 4, 1.0)]),
                       tab_1d, ROT_DIMS // 2, 'k', out_dtype=F32)
        q_d = qk_prep(proj, blk('dq', 4), _pair_weights([d_q_norm[i]] * 4, scale2), tab_1d,
                      ROT_DIMS // 2, 'q')
        k_d = qk_prep(proj, blk('dk', 4), _pair_weights([d_k_norm[i]] * 4, 1.0), tab_1d,
                      ROT_DIMS // 2, 'k')
        q1 = qk_prep(proj, blk('cq', 4), _pair_weights([c_q_norm[i]] * 4, scale2), tab_ax,
                     HEAD_DIM // 4, 'q')
        k1 = qk_prep(proj, blk('ck', 1), _pair_weights([c_k_norm[i]], 1.0), tab_ax, HEAD_DIM // 4, 'kdup')

        a_out = band_attn(qk_a, proj[:, OFF['av']:OFF['av'] + 512].astype(F32))

        b_out = mixer_b(proj, s5_lambda_re[i], s5_lambda_im[i], s5_log_dt[i], s5_b_re[i], s5_b_im[i],
                        s5_c_re[i], s5_c_im[i], s5_d[i], s5_w_glu[i], s5_b_glu[i])

        cv = proj[:, OFF['cv']:OFF['cv'] + LANES]
        ones = jnp.ones((s, HEAD_DIM), BF16)
        v_ones = jnp.stack([jnp.concatenate([cv[:, :HEAD_DIM], ones], axis=1),
                            jnp.concatenate([cv[:, HEAD_DIM:], ones], axis=1)])
        c_out = gqa_attn(q1, k1, v_ones)

        lam_init = 0.8 - 0.6 * math.exp(-0.3 * i)
        pad = lambda v: jnp.pad(v.astype(F32), (0, LANES - HEAD_DIM))
        lam_rows = jnp.stack([pad(d_lambda_q1[i]), pad(d_lambda_k1[i]), pad(d_lambda_q2[i]),
                              pad(d_lambda_k2[i]), jnp.full((LANES,), lam_init, F32)]
                             + [jnp.zeros((LANES,), F32)] * 3)
        dv = proj[:, OFF['dv']:OFF['dv'] + 512].reshape(s, 4, LANES)
        dv_ones = jnp.concatenate([dv, jnp.ones_like(dv)], axis=2).reshape(s, 8 * LANES)
        d_out = diff_attn(q_d, k_d, dv_ones, lam_rows, d_subln[i])

        xs = out_ple(xs, a_out, b_out, c_out, d_out, proj, p[i, 0], w_out[i].astype(BF16),
                     ple_norm_w[i], ple_gate_w[i].astype(BF16), ple_w[i].astype(BF16))
    return xs[None]
```

```python
import functools
import math

import jax
import jax.numpy as jnp
import numpy as np
from jax import lax
from jax.experimental import pallas as pl
from jax.experimental.pallas import tpu as pltpu

F32 = jnp.float32
BF16 = jnp.bfloat16

HEAD_DIM = 64
LANES = 128
NORM_EPS = 1e-6
MASK_VALUE = -1e30
ROPE_THETA = 500000.0
AXIAL_THETA = 10000.0
ROT_DIMS = HEAD_DIM // 4
GRID_W = 64
DILATIONS = (1, 4, 16)
N_SIDE = 64
SSM_GROUP = 16
SSM_STATE = 64
N_SEG = 8
FLASH_ROWS = 64
FLASH_CHUNKS_PER_TRIP = 16
VMEM_LIMIT = 56 * 1024 * 1024

_ORIG = dict(aq=(0, 512), ak=(512, 512), av=(1024, 512), ag=(1536, 512), bu=(2048, 512),
             bg=(2560, 512), cq=(3072, 512), ck=(3584, 128), cv=(3712, 128), cg=(3840, 512),
             dq=(4352, 512), dk=(4864, 512), dv=(5376, 512), dg=(5888, 512))
_ORDER = ('aq', 'ak', 'dq', 'dk', 'cq', 'av', 'dv', 'bu', 'ag', 'bg', 'cg', 'dg', 'ck', 'cv')
OFF = {}
_o = 0
for _n in _ORDER:
    OFF[_n] = _o
    _o += _ORIG[_n][1]
IN_COLS = _o


def _params(*sem):
    return pltpu.CompilerParams(dimension_semantics=sem, vmem_limit_bytes=VMEM_LIMIT)


def _rms(x, w):
    return x * lax.rsqrt(jnp.mean(x * x, axis=-1, keepdims=True) + NORM_EPS) * w


def _in_proj_kernel(x_ref, nw_ref, w_ref, o_ref, h_ref):
    @pl.when(pl.program_id(1) == 0)
    def _():
        h_ref[...] = _rms(x_ref[...], nw_ref[...]).astype(BF16)

    o_ref[...] = jnp.dot(h_ref[...], w_ref[...], preferred_element_type=F32).astype(o_ref.dtype)


def in_proj(x, norm_w, w, tm=1024, tn=1280):
    s, d = x.shape
    n = w.shape[1]
    tm = min(tm, s)
    return pl.pallas_call(
        _in_proj_kernel,
        grid=(s // tm, n // tn),
        in_specs=[pl.BlockSpec((tm, d), lambda i, j: (i, 0)),
                  pl.BlockSpec((1, d), lambda i, j: (0, 0)),
                  pl.BlockSpec((d, tn), lambda i, j: (0, j))],
        out_specs=pl.BlockSpec((tm, tn), lambda i, j: (i, j)),
        out_shape=jax.ShapeDtypeStruct((s, n), BF16),
        scratch_shapes=[pltpu.VMEM((tm, d), BF16)],
        compiler_params=_params("parallel", "arbitrary"),
        name="in_proj",
    )(x, norm_w.reshape(1, d), w)


def _qk_prep_kernel(cb_ref, x_ref, w_ref, g_ref, t_ref, o_ref, *, shift, mode):
    del cb_ref
    x = x_ref[...].astype(F32)
    ms = jnp.dot((x * x).astype(BF16), g_ref[...], preferred_element_type=F32)
    y = x * lax.rsqrt(ms + NORM_EPS) * w_ref[...]
    y = (y * t_ref[0] + pltpu.roll(y, shift, 1) * t_ref[1]
         + pltpu.roll(y, LANES - shift, 1) * t_ref[2])
    first = lax.broadcasted_iota(jnp.int32, y.shape, 1) < HEAD_DIM
    if mode == 'q':
        o_ref[0] = jnp.where(first, y, 0.0).astype(o_ref.dtype)
        o_ref[1] = jnp.where(first, 0.0, y).astype(o_ref.dtype)
    elif mode == 'k':
        o_ref[...] = y.astype(o_ref.dtype)
    else:
        r = pltpu.roll(y, HEAD_DIM, 1)
        o_ref[0] = jnp.where(first, y, r).astype(o_ref.dtype)
        o_ref[1] = jnp.where(first, r, y).astype(o_ref.dtype)


def qk_prep(proj, col_blocks, weights, table, shift, mode, out_dtype=BF16, tm=2048):
    s = proj.shape[0]
    tm = min(tm, s)
    nb = len(col_blocks)
    cb = jnp.asarray(col_blocks, jnp.int32)
    gmat = jnp.asarray(np.kron(np.eye(2), np.full((HEAD_DIM, HEAD_DIM), 1.0 / HEAD_DIM)), BF16)
    if mode == 'k':
        out_shape = jax.ShapeDtypeStruct((s, nb * LANES), out_dtype)
        out_spec = pl.BlockSpec((tm, LANES), lambda i, j, cb: (i, j))
    else:
        out_shape = jax.ShapeDtypeStruct((2 * nb, s, LANES), out_dtype)
        out_spec = pl.BlockSpec((2, tm, LANES), lambda i, j, cb: (j, i, 0))
    grid_spec = pltpu.PrefetchScalarGridSpec(
        num_scalar_prefetch=1,
        grid=(s // tm, nb),
        in_specs=[pl.BlockSpec((tm, LANES), lambda i, j, cb: (i, cb[j])),
                  pl.BlockSpec((None, 1, LANES), lambda i, j, cb: (j, 0, 0)),
                  pl.BlockSpec((LANES, LANES), lambda i, j, cb: (0, 0)),
                  pl.BlockSpec((3, tm, LANES), lambda i, j, cb: (0, i, 0))],
        out_specs=out_spec,
    )
    return pl.pallas_call(
        functools.partial(_qk_prep_kernel, shift=shift, mode=mode),
        grid_spec=grid_spec,
        out_shape=out_shape,
        compiler_params=_params("parallel", "arbitrary"),
        name="qk_prep_" + mode,
    )(cb, proj, weights, gmat, table)


def _rope_tables(pos, n_dims, theta):
    inv = theta ** (-jnp.arange(0, n_dims, 2, dtype=F32) / n_dims)
    ang = pos[:, None] * inv[None, :]
    return jnp.cos(ang), jnp.sin(ang)


def _rope_table_1d(s):
    cos, sin = _rope_tables(jnp.arange(s).astype(F32), ROT_DIMS, ROPE_THETA)
    z8 = jnp.zeros_like(cos)
    rest = HEAD_DIM - ROT_DIMS
    c = jnp.concatenate([cos, cos, jnp.ones((s, rest), F32)], axis=1)
    s1 = jnp.concatenate([z8, sin, jnp.zeros((s, rest), F32)], axis=1)
    s2 = jnp.concatenate([-sin, z8, jnp.zeros((s, rest), F32)], axis=1)
    return jnp.stack([jnp.tile(t, (1, 2)) for t in (c, s1, s2)])


def _rope_table_axial(s):
    t = jnp.arange(s)
    rows = s // GRID_W
    row_c = (t // GRID_W - rows // 2).astype(F32)
    col_c = (t % GRID_W - GRID_W // 2).astype(F32)
    cr, sr = _rope_tables(row_c, HEAD_DIM // 2, AXIAL_THETA)
    cc, sc = _rope_tables(col_c, HEAD_DIM // 2, AXIAL_THETA)
    z = jnp.zeros_like(cr)
    c = jnp.concatenate([cr, cr, cc, cc], axis=1)
    s1 = jnp.concatenate([z, sr, z, sc], axis=1)
    s2 = jnp.concatenate([-sr, z, -sc, z], axis=1)
    return jnp.stack([jnp.tile(t_, (1, 2)) for t_ in (c, s1, s2)])


def _band_attn_kernel(q_ref, k_ref, v_ref, o_ref, os_ref, ls_ref, bias_ref, *, tq, comb_rows):
    n_rows = k_ref.shape[0]
    span = q_ref.shape[0]
    win = tq + 2 * N_SIDE
    sp = pl.program_id(1)
    first = lax.broadcasted_iota(jnp.int32, (tq, LANES), 1) < HEAD_DIM
    row = lax.broadcasted_iota(jnp.int32, (2 * tq, win), 0)
    row = jnp.where(row >= tq, row - tq, row)
    rel = lax.broadcasted_iota(jnp.int32, (2 * tq, win), 1) - row
    for case in range(3):
        bias_ref[case] = jnp.where(jnp.abs(rel - case * N_SIDE) <= N_SIDE, 0.0, MASK_VALUE)

    for pi, d in enumerate(DILATIONS):
        seg_len = n_rows // d
        per_class = span // (tq * d)

        def tile(tid, carry, pi=pi, d=d, seg_len=seg_len, per_class=per_class):
            r = tid // per_class
            ti = tid % per_class
            i0 = sp * (span // d) + ti * tq
            c0 = jnp.clip(i0 - N_SIDE, 0, seg_len - win)
            q_rows = pl.ds(r + d * ti * tq, tq, stride=d)
            k_rows = pl.ds(r + d * c0, win, stride=d)
            q = q_ref[q_rows, :]
            q2 = jnp.concatenate([jnp.where(first, q, 0.0), jnp.where(first, 0.0, q)], axis=0)
            kw = k_ref[k_rows, :].astype(BF16)
            vw = v_ref[k_rows, :].astype(BF16)
            s = lax.dot_general(q2.astype(BF16), kw, (((1,), (1,)), ((), ())),
                                preferred_element_type=F32)
            s = s + bias_ref[(i0 - c0) // N_SIDE]
            m = jnp.max(s, axis=-1, keepdims=True)
            p = jnp.exp(s - m)
            l = jnp.sum(p, axis=-1, keepdims=True)
            o = jnp.dot(p.astype(BF16), vw, preferred_element_type=F32) / l
            lse = m + jnp.log(l)
            os_ref[pi, q_rows, :] = jnp.where(first, o[:tq], o[tq:])
            ls_ref[pi, q_rows, :] = jnp.where(first, lse[:tq], lse[tq:])
            return carry

        lax.fori_loop(0, span // tq, tile, 0, unroll=16)

    def combine(c, carry):
        rows = pl.ds(pl.multiple_of(c * comb_rows, comb_rows), comb_rows)
        la, lb, lc = ls_ref[0, rows, :], ls_ref[1, rows, :], ls_ref[2, rows, :]
        m = jnp.maximum(jnp.maximum(la, lb), lc)
        ea, eb, ec = jnp.exp(la - m), jnp.exp(lb - m), jnp.exp(lc - m)
        num = ea * os_ref[0, rows, :] + eb * os_ref[1, rows, :] + ec * os_ref[2, rows, :]
        o_ref[rows, :] = (num / (ea + eb + ec)).astype(o_ref.dtype)
        return carry

    lax.fori_loop(0, span // comb_rows, combine, 0)


def band_attn(qk, v, tq=128, comb_rows=256):
    s = qk.shape[0]
    n_pairs = v.shape[1] // LANES
    span = tq * max(DILATIONS)
    assert s % span == 0 and s // max(DILATIONS) >= tq + 2 * N_SIDE
    return pl.pallas_call(
        functools.partial(_band_attn_kernel, tq=tq, comb_rows=comb_rows),
        grid=(n_pairs, s // span),
        in_specs=[pl.BlockSpec((span, LANES), lambda m, c: (c, m)),
                  pl.BlockSpec((s, LANES), lambda m, c: (0, n_pairs + m)),
                  pl.BlockSpec((s, LANES), lambda m, c: (0, m))],
        out_specs=pl.BlockSpec((span, LANES), lambda m, c: (c, m)),
        out_shape=jax.ShapeDtypeStruct((s, n_pairs * LANES), BF16),
        scratch_shapes=[pltpu.VMEM((len(DILATIONS), span, LANES), F32)] * 2
        + [pltpu.VMEM((3, 2 * tq, tq + 2 * N_SIDE), F32)],
        compiler_params=_params("parallel", "arbitrary"),
        name="band_attn",
    )(qk, qk, v)


def _s5_disc_kernel(lr_ref, li_ref, ldt_ref, br_ref, bi_ref, ar_ref, ai_ref, bbr_ref, bbi_ref):
    lr, li = lr_ref[...], li_ref[...]
    dt = jnp.exp(ldt_ref[...])
    mag = jnp.exp(lr * dt)
    ar = mag * jnp.cos(li * dt)
    ai = mag * jnp.sin(li * dt)
    den = lr * lr + li * li
    cre = ((ar - 1.0) * lr + ai * li) / den
    cim = (ai * lr - (ar - 1.0) * li) / den
    br, bi = br_ref[...], bi_ref[...]
    ar_ref[...] = ar
    ai_ref[...] = ai
    bbr_ref[...] = cre * br - cim * bi
    bbi_ref[...] = cre * bi + cim * br


def s5_discretise(lam_re, lam_im, log_dt, b_re, b_im):
    two, g, p = lam_re.shape
    c = b_re.shape[-1]
    rep = lambda t: jnp.repeat(t.reshape(two * g, p), c, axis=1)
    shp = jax.ShapeDtypeStruct((two * g, p * c), F32)
    ar, ai, bbr, bbi = pl.pallas_call(
        _s5_disc_kernel,
        out_shape=[shp] * 4,
        name="s5_disc",
    )(rep(lam_re), rep(lam_im), log_dt.reshape(two * g, 1),
      b_re.reshape(two * g, p * c), b_im.reshape(two * g, p * c))
    unrep = lambda t: t.reshape(two, g, p, c)[..., 0]
    return unrep(ar), unrep(ai), bbr.reshape(two, g, p, c), bbi.reshape(two, g, p, c)


def _block_diag(m):
    g, r, c = m.shape
    eye = jnp.eye(g, dtype=m.dtype)
    return jnp.einsum('grc,gh->grhc', m, eye).reshape(g * r, g * c)


def _s5_scan_kernel(u_ref, bre_ref, bim_ref, ar_ref, ai_ref, x0r_ref, x0i_ref, *rest,
                    emit, lane_blk):
    if emit:
        cre_ref, cim_ref, y_ref, xr_ref, xi_ref, sr_ref, si_ref = rest
    else:
        er_ref, ei_ref, xr_ref, xi_ref, sr_ref, si_ref = rest
    back = pl.program_id(0)
    ic = pl.program_id(1)
    n_i = u_ref.shape[0] // N_SEG
    n_state = xr_ref.shape[1]

    @pl.when(ic == 0)
    def _():
        sr_ref[...] = x0r_ref[...]
        si_ref[...] = x0i_ref[...]

    n_blk = n_state // lane_blk
    ch_blk = u_ref.shape[1] // n_blk

    def run(reverse):
        for lb in range(n_blk):
            cols = slice(lb * lane_blk, (lb + 1) * lane_blk)
            ch = slice(lb * ch_blk, (lb + 1) * ch_blk)
            u = u_ref[:, ch]
            xr_ref[:, cols] = jnp.dot(u, bre_ref[ch, cols], preferred_element_type=F32)
            xi_ref[:, cols] = jnp.dot(u, bim_ref[ch, cols], preferred_element_type=F32)
            ar = jnp.broadcast_to(ar_ref[:, cols], (N_SEG, lane_blk))
            ai = jnp.broadcast_to(ai_ref[:, cols], (N_SEG, lane_blk))
            xr, xi = sr_ref[:, cols], si_ref[:, cols]
            for i in (range(n_i - 1, -1, -1) if reverse else range(n_i)):
                rows = slice(i * N_SEG, (i + 1) * N_SEG)
                xr, xi = (ar * xr - ai * xi + xr_ref[rows, cols],
                          ar * xi + ai * xr + xi_ref[rows, cols])
                if emit:
                    xr_ref[rows, cols] = xr
                    xi_ref[rows, cols] = xi
            sr_ref[:, cols] = xr
            si_ref[:, cols] = xi
            if emit:
                y_ref[:, ch] = (
                    jnp.dot(xr_ref[:, cols].astype(BF16), cre_ref[cols, ch], preferred_element_type=F32)
                    - jnp.dot(xi_ref[:, cols].astype(BF16), cim_ref[cols, ch], preferred_element_type=F32))

    pl.when(back == 0)(lambda: run(False))
    pl.when(back == 1)(lambda: run(True))

    if not emit:
        @pl.when(ic == pl.num_programs(1) - 1)
        def _():
            er_ref[...] = sr_ref[...]
            ei_ref[...] = si_ref[...]


def s5_scan(u8, bmat_re, bmat_im, a_re, a_im, x0_re, x0_im, cmat_re=None, cmat_im=None,
            rows=256, lane_blk=512):
    s, w = u8.shape
    two = bmat_re.shape[0]
    n_state = bmat_re.shape[-1]
    rows = min(rows, s)
    nblk = s // rows
    emit = cmat_re is not None
    dspec = lambda shape: pl.BlockSpec((None,) + shape, lambda d, i: (d,) + (0,) * len(shape))
    blk = lambda d, i: i + d * (nblk - 1 - 2 * i)
    in_specs = [pl.BlockSpec((rows, w), lambda d, i: (blk(d, i), 0)),
                dspec((w, n_state)), dspec((w, n_state)),
                dspec((1, n_state)), dspec((1, n_state)),
                dspec((N_SEG, n_state)), dspec((N_SEG, n_state))]
    args = [u8, bmat_re, bmat_im, a_re, a_im, x0_re, x0_im]
    if emit:
        in_specs += [dspec((n_state, w)), dspec((n_state, w))]
        args += [cmat_re, cmat_im]
        out_specs = pl.BlockSpec((None, rows, w), lambda d, i: (d, blk(d, i), 0))
        out_shape = jax.ShapeDtypeStruct((two, s, w), F32)
    else:
        out_specs = [dspec((N_SEG, n_state))] * 2
        out_shape = [jax.ShapeDtypeStruct((two, N_SEG, n_state), F32)] * 2
    return pl.pallas_call(
        functools.partial(_s5_scan_kernel, emit=emit, lane_blk=lane_blk),
        grid=(two, s // rows),
        in_specs=in_specs,
        out_specs=out_specs,
        out_shape=out_shape,
        scratch_shapes=[pltpu.VMEM((rows, n_state), F32), pltpu.VMEM((rows, n_state), F32),
                        pltpu.VMEM((N_SEG, n_state), F32), pltpu.VMEM((N_SEG, n_state), F32)],
        compiler_params=_params("parallel", "arbitrary"),
        name="s5_scan_emit" if emit else "s5_scan_ends",
    )(*args)


def _s5_carry_kernel(er_ref, ei_ref, ar_ref, ai_ref, cr_ref, ci_ref, *, seg_len):
    assert seg_len & (seg_len - 1) == 0
    for d in range(er_ref.shape[0]):
        pr, pi = ar_ref[d], ai_ref[d]
        n = seg_len
        while n > 1:
            pr, pi = pr * pr - pi * pi, 2.0 * pr * pi
            n //= 2
        order = range(N_SEG) if d == 0 else range(N_SEG - 1, -1, -1)
        prev = None
        for j in order:
            if prev is None:
                cr = jnp.zeros_like(pr)
                ci = jnp.zeros_like(pi)
            else:
                cr, ci = (er_ref[d, prev:prev + 1, :] + pr * cr - pi * ci,
                          ei_ref[d, prev:prev + 1, :] + pr * ci + pi * cr)
            cr_ref[d, j:j + 1, :] = cr
            ci_ref[d, j:j + 1, :] = ci
            prev = j


def s5_carry(e_re, e_im, a_re, a_im, seg_len):
    shp = jax.ShapeDtypeStruct(e_re.shape, F32)
    return pl.pallas_call(
        functools.partial(_s5_carry_kernel, seg_len=seg_len),
        out_shape=[shp, shp],
        name="s5_carry",
    )(e_re, e_im, a_re, a_im)


def _s5_glu_kernel(u_ref, yf_ref, yb_ref, d_ref, w_ref, b_ref, o_ref):
    width = o_ref.shape[1]
    y = d_ref[...] * u_ref[...].astype(F32) + yf_ref[...] + yb_ref[...]
    c = math.sqrt(2.0 / math.pi)
    y = 0.5 * y * (1.0 + jnp.tanh(c * (y + 0.044715 * (y * y * y))))
    z = jnp.dot(y.astype(BF16), w_ref[...], preferred_element_type=F32) + b_ref[...]
    o_ref[...] = (z[:, :width] * jax.nn.sigmoid(z[:, width:])).astype(o_ref.dtype)


def s5_glu(u, y, d_skip, w_glu, b_glu, tm=1024):
    s, w = u.shape
    tm = min(tm, s)
    return pl.pallas_call(
        _s5_glu_kernel,
        grid=(s // tm,),
        in_specs=[pl.BlockSpec((tm, w), lambda i: (i, 0)),
                  pl.BlockSpec((None, tm, w), lambda i: (0, i, 0)),
                  pl.BlockSpec((None, tm, w), lambda i: (1, i, 0)),
                  pl.BlockSpec((1, w), lambda i: (0, 0)),
                  pl.BlockSpec((w, 2 * w), lambda i: (0, 0)),
                  pl.BlockSpec((1, 2 * w), lambda i: (0, 0))],
        out_specs=pl.BlockSpec((tm, w), lambda i: (i, 0)),
        out_shape=jax.ShapeDtypeStruct((s, w), BF16),
        compiler_params=_params("parallel"),
        name="s5_glu",
    )(u, y, y, d_skip.reshape(1, w), w_glu, b_glu.reshape(1, 2 * w))


def _to_segments(x):
    *lead, s, w = x.shape
    return x.reshape(*lead, N_SEG, s // N_SEG, w).swapaxes(-3, -2).reshape(*lead, s, w)


def _from_segments(x):
    *lead, s, w = x.shape
    return x.reshape(*lead, s // N_SEG, N_SEG, w).swapaxes(-3, -2).reshape(*lead, s, w)


def mixer_b(proj, lam_re, lam_im, log_dt, b_re, b_im, c_re, c_im, d_skip, w_glu, b_glu):
    s = proj.shape[0]
    two, g, p = lam_re.shape
    a_re, a_im, bb_re, bb_im = s5_discretise(lam_re, lam_im, log_dt, b_re, b_im)
    bmat_re = jnp.stack([_block_diag(bb_re[d].swapaxes(1, 2)) for d in range(two)]).astype(BF16)
    bmat_im = jnp.stack([_block_diag(bb_im[d].swapaxes(1, 2)) for d in range(two)]).astype(BF16)
    cmat_re = jnp.stack([_block_diag(c_re[d].swapaxes(1, 2)) for d in range(two)]).astype(BF16)
    cmat_im = jnp.stack([_block_diag(c_im[d].swapaxes(1, 2)) for d in range(two)]).astype(BF16)
    a_re = a_re.reshape(two, 1, g * p)
    a_im = a_im.reshape(two, 1, g * p)
    u8 = _to_segments(proj[:, OFF['bu']:OFF['bu'] + 512])
    zeros = jnp.zeros((two, N_SEG, g * p), F32)
    e_re, e_im = s5_scan(u8, bmat_re, bmat_im, a_re, a_im, zeros, zeros)
    x0_re, x0_im = s5_carry(e_re, e_im, a_re, a_im, s // N_SEG)
    y8 = s5_scan(u8, bmat_re, bmat_im, a_re, a_im, x0_re, x0_im, cmat_re, cmat_im)
    return _from_segments(s5_glu(u8, y8, d_skip, w_glu.astype(BF16), b_glu))


def _flash_scores(q_ref, k_ref, kc, tk, s_ref, slot):
    r = s_ref.shape[1]
    rows = pl.ds(pl.multiple_of(kc * tk, tk), tk)
    q = q_ref[...].reshape(r, LANES)
    s_ref[slot] = lax.dot_general(q, k_ref[rows, :], (((1,), (1,)), ((), ())),
                                  preferred_element_type=F32)


def _flash_update(v_ref, kc, tk, s_ref, slot, p_ref, m_ref, acc_ref):
    rows = pl.ds(pl.multiple_of(kc * tk, tk), tk)
    for rb in range(m_ref.shape[0] // FLASH_ROWS):
        rs = slice(rb * FLASH_ROWS, (rb + 1) * FLASH_ROWS)
        tiles = [s_ref[slot, rs, j * LANES:(j + 1) * LANES] for j in range(tk // LANES)]
        m_cur = jnp.max(functools.reduce(jnp.maximum, tiles), axis=-1, keepdims=True)
        m_prev = m_ref[rs]
        m_new = jnp.maximum(m_prev, jnp.broadcast_to(m_cur, m_prev.shape))
        alpha = jnp.exp2(m_prev - m_new)
        p_ref[rs] = jnp.concatenate([jnp.exp2((t - m_new).astype(BF16)) for t in tiles], axis=1)
        acc_ref[rs] = jnp.tile(alpha, (1, acc_ref.shape[1] // LANES)) * acc_ref[rs]
        m_ref[rs] = m_new
    acc_ref[...] += jnp.dot(p_ref[...], v_ref[rows, :], preferred_element_type=F32)


def _flash(q_ref, k_ref, v_ref, tk, s_ref, p_ref, m_ref, acc_ref):
    m_ref[...] = jnp.full(m_ref.shape, -jnp.inf, F32)
    acc_ref[...] = jnp.zeros(acc_ref.shape, F32)
    nk = k_ref.shape[0] // tk
    per_trip = min(FLASH_CHUNKS_PER_TRIP, nk)
    assert nk % per_trip == 0 and per_trip % 2 == 0
    _flash_scores(q_ref, k_ref, 0, tk, s_ref, 0)

    def body(j, carry):
        for c in range(per_trip):
            kc = per_trip * j + c
            nxt = kc + 1 if c + 1 < per_trip else jnp.minimum(kc + 1, nk - 1)
            _flash_scores(q_ref, k_ref, nxt, tk, s_ref, (c + 1) % 2)
            _flash_update(v_ref, kc, tk, s_ref, c % 2, p_ref, m_ref, acc_ref)
        return carry

    lax.fori_loop(0, nk // per_trip, body, 0)
    return acc_ref[...]


def _gqa_kernel(q_ref, k_ref, v_ref, o_ref, s_ref, p_ref, m_ref, acc_ref, *, tk):
    nh, tq, _ = q_ref.shape
    acc = _flash(q_ref, k_ref, v_ref, tk, s_ref, p_ref, m_ref, acc_ref)
    o = acc / pltpu.roll(acc, HEAD_DIM, 1)
    first = lax.broadcasted_iota(jnp.int32, (tq, LANES), 1) < HEAD_DIM
    for pair in range(nh // 2):
        even = o[(2 * pair) * tq:(2 * pair + 1) * tq]
        odd = pltpu.roll(o[(2 * pair + 1) * tq:(2 * pair + 2) * tq], HEAD_DIM, 1)
        o_ref[:, pair * LANES:(pair + 1) * LANES] = jnp.where(first, even, odd).astype(o_ref.dtype)


def gqa_attn(q_pad, k_dup, v_ones, tq=256, tk=1024):
    nh, s, _ = q_pad.shape
    nkv = k_dup.shape[0]
    grp = nh // nkv
    tq, tk = min(tq, s), min(tk, s)
    r = grp * tq
    return pl.pallas_call(
        functools.partial(_gqa_kernel, tk=tk),
        grid=(nkv, s // tq),
        in_specs=[pl.BlockSpec((grp, tq, LANES), lambda g, i: (g, i, 0)),
                  pl.BlockSpec((None, s, LANES), lambda g, i: (g, 0, 0)),
                  pl.BlockSpec((None, s, LANES), lambda g, i: (g, 0, 0))],
        out_specs=pl.BlockSpec((tq, grp * HEAD_DIM), lambda g, i: (i, g)),
        out_shape=jax.ShapeDtypeStruct((s, nh * HEAD_DIM), BF16),
        scratch_shapes=[pltpu.VMEM((2, r, tk), F32), pltpu.VMEM((r, tk), BF16)]
        + [pltpu.VMEM((r, LANES), F32)] * 2,
        compiler_params=_params("parallel", "arbitrary"),
        name="gqa_attn",
    )(q_pad, k_dup, v_ones)


def _diff_kernel(q_ref, k_ref, v_ref, lam_ref, w_ref, o_ref, s_ref, p_ref, m_ref, acc_ref, *, tk):
    _, tq, _ = q_ref.shape
    acc = _flash(q_ref, k_ref, v_ref, tk, s_ref, p_ref, m_ref, acc_ref)
    o = acc[:, :LANES] / acc[:, LANES:]
    lp = lam_ref[...]
    lam_init = lp[4:5, 0:1]
    lam = (jnp.exp(jnp.sum(lp[0:1] * lp[1:2], axis=-1, keepdims=True))
           - jnp.exp(jnp.sum(lp[2:3] * lp[3:4], axis=-1, keepdims=True)) + lam_init)
    d = o[:tq] - lam * o[tq:]
    o_ref[...] = (_rms(d, w_ref[...]) * (1.0 - lam_init)).astype(o_ref.dtype)


def diff_attn(q_pad, k, v_ones, lam_rows, subln_w, tq=512, tk=1024):
    n2, s, _ = q_pad.shape
    nh = n2 // 2
    tq, tk = min(tq, s), min(tk, s)
    return pl.pallas_call(
        functools.partial(_diff_kernel, tk=tk),
        grid=(nh, s // tq),
        in_specs=[pl.BlockSpec((2, tq, LANES), lambda h, i: (h, i, 0)),
                  pl.BlockSpec((s, LANES), lambda h, i: (0, h)),
                  pl.BlockSpec((s, 2 * LANES), lambda h, i: (0, h)),
                  pl.BlockSpec((8, LANES), lambda h, i: (0, 0)),
                  pl.BlockSpec((1, LANES), lambda h, i: (0, 0))],
        out_specs=pl.BlockSpec((tq, LANES), lambda h, i: (i, h)),
        out_shape=jax.ShapeDtypeStruct((s, nh * LANES), BF16),
        scratch_shapes=[pltpu.VMEM((2, 2 * tq, tk), F32), pltpu.VMEM((2 * tq, tk), BF16),
                        pltpu.VMEM((2 * tq, LANES), F32), pltpu.VMEM((2 * tq, 2 * LANES), F32)],
        compiler_params=_params("parallel", "arbitrary"),
        name="diff_attn",
    )(q_pad, k, v_ones, lam_rows, subln_w.reshape(1, LANES))


def _out_ple_kernel(x_ref, a_ref, b_ref, c_ref, d_ref, g_ref, p_ref, wo_ref, nw_ref, gw_ref, pw_ref,
                    o_ref):
    g = g_ref[...].astype(F32)
    mix = jnp.concatenate([a_ref[...], b_ref[...], c_ref[...], d_ref[...]], axis=-1).astype(F32)
    mixed = (mix * (g * jax.nn.sigmoid(g))).astype(BF16)
    x1 = x_ref[...] + jnp.dot(mixed, wo_ref[...], preferred_element_type=F32)
    h = _rms(x1, nw_ref[...]).astype(BF16)
    gate = jax.nn.sigmoid(jnp.dot(h, gw_ref[...], preferred_element_type=F32))
    pe = jnp.dot(p_ref[...].astype(BF16), pw_ref[...], preferred_element_type=F32)
    o_ref[...] = x1 + gate * pe


def out_ple(x, a, b, c, d, proj, p, w_out, ple_norm_w, ple_gate_w, ple_w, tm=256):
    s, dm = x.shape
    bw = a.shape[1]
    pd = p.shape[1]
    tm = min(tm, s)
    gb = OFF['ag'] // dm
    row = lambda w: pl.BlockSpec((tm, w), lambda i: (i, 0))
    const = lambda r, c_: pl.BlockSpec((r, c_), lambda i: (0, 0), pipeline_mode=pl.Buffered(1))
    return pl.pallas_call(
        _out_ple_kernel,
        grid=(s // tm,),
        in_specs=[row(dm), row(bw), row(bw), row(bw), row(bw),
                  pl.BlockSpec((tm, dm), lambda i: (i, gb)), row(pd),
                  const(dm, dm), const(1, dm), const(dm, dm), const(pd, dm)],
        out_specs=row(dm),
        out_shape=jax.ShapeDtypeStruct((s, dm), F32),
        compiler_params=_params("parallel"),
        name="out_ple",
    )(x, a, b, c, d, proj, p, w_out, ple_norm_w.reshape(1, dm), ple_gate_w, ple_w)


def _reorder_cols(w):
    return jnp.concatenate([w[:, _ORIG[n][0]:_ORIG[n][0] + _ORIG[n][1]] for n in _ORDER], axis=1)


def _pair_weights(ws, scale):
    return jnp.stack([jnp.tile(w.astype(F32) * scale, 2).reshape(1, LANES) for w in ws])


def kernel(x, p, norm_w, w_in, w_out, a_q_norm, a_k_norm, s5_lambda_re, s5_lambda_im, s5_log_dt, s5_b_re, s5_b_im, s5_c_re, s5_c_im, s5_d, s5_w_glu, s5_b_glu, c_q_norm, c_k_norm, d_q_norm, d_k_norm, d_lambda_q1, d_lambda_k1, d_lambda_q2, d_lambda_k2, d_subln, ple_norm_w, ple_gate_w, ple_w):
    bsz, s, dm = x.shape
    assert bsz == 1
    depth = w_in.shape[0]
    scale = HEAD_DIM ** -0.5
    scale2 = scale * math.log2(math.e)
    tab_1d = _rope_table_1d(s)
    tab_ax = _rope_table_axial(s)
    blk = lambda name, n: tuple(OFF[name] // LANES + j for j in range(n))
    xs = x[0]
    for i in range(depth):
        proj = in_proj(xs, norm_w[i], _reorder_cols(w_in[i].astype(BF16)))

        qk_a = qk_prep(proj, blk('aq', 4) + blk('ak', 4),
                       jnp.concatenate([_pair_weights([a_q_norm[i]] * 4, scale),
                                        _pair_weights([a_k_norm[i]] * 4, 1.0)]),
                       tab_1d, ROT_DIMS // 2, 'k', out_dtype=F32)
        q_d = qk_prep(proj, blk('dq', 4), _pair_weights([d_q_norm[i]] * 4, scale2), tab_1d,
                      ROT_DIMS // 2, 'q')
        k_d = qk_prep(proj, blk('dk', 4), _pair_weights([d_k_norm[i]] * 4, 1.0), tab_1d,
                      ROT_DIMS // 2, 'k')
        q1 = qk_prep(proj, blk('cq', 4), _pair_weights([c_q_norm[i]] * 4, scale2), tab_ax,
                     HEAD_DIM // 4, 'q')
        k1 = qk_prep(proj, blk('ck', 1), _pair_weights([c_k_norm[i]], 1.0), tab_ax, HEAD_DIM // 4, 'kdup')

        a_out = band_attn(qk_a, proj[:, OFF['av']:OFF['av'] + 512].astype(F32))

        b_out = mixer_b(proj, s5_lambda_re[i], s5_lambda_im[i], s5_log_dt[i], s5_b_re[i], s5_b_im[i],
                        s5_c_re[i], s5_c_im[i], s5_d[i], s5_w_glu[i], s5_b_glu[i])

        cv = proj[:, OFF['cv']:OFF['cv'] + LANES]
        ones = jnp.ones((s, HEAD_DIM), BF16)
        v_ones = jnp.stack([jnp.concatenate([cv[:, :HEAD_DIM], ones], axis=1),
                            jnp.concatenate([cv[:, HEAD_DIM:], ones], axis=1)])
        c_out = gqa_attn(q1, k1, v_ones)

        lam_init = 0.8 - 0.6 * math.exp(-0.3 * i)
        pad = lambda v: jnp.pad(v.astype(F32), (0, LANES - HEAD_DIM))
        lam_rows = jnp.stack([pad(d_lambda_q1[i]), pad(d_lambda_k1[i]), pad(d_lambda_q2[i]),
                              pad(d_lambda_k2[i]), jnp.full((LANES,), lam_init, F32)]
                             + [jnp.zeros((LANES,), F32)] * 3)
        dv = proj[:, OFF['dv']:OFF['dv'] + 512].reshape(s, 4, LANES)
        dv_ones = jnp.concatenate([dv, jnp.ones_like(dv)], axis=2).reshape(s, 8 * LANES)
        d_out = diff_attn(q_d, k_d, dv_ones, lam_rows, d_subln[i])

        xs = out_ple(xs, a_out, b_out, c_out, d_out, proj, p[i, 0], w_out[i].astype(BF16),
                     ple_norm_w[i], ple_gate_w[i].astype(BF16), ple_w[i].astype(BF16))
    return xs[None]
```
